```python
import math
import jax
import jax.numpy as jnp
from jax import lax
import numpy as np

D_MODEL = 2048
BATCH = 2
SEQ = 8192
DEPTH = 1
DEC_BATCH = 32
DEC_SEQ = 1
PAST_LEN = 16384
PAGE_SIZE = 128

NSA_HEADS = 8
NSA_KV = 2
NSA_HG = NSA_HEADS // NSA_KV
HD = 128
CMP_BLOCK = 32
CMP_STRIDE = 16
CMP_HID = 128
SLC_BLOCK = 64
SLC_TOP = 16
WINDOW = 512
QBLOCK = 128
ROPE_DIM = HD // 4
ROPE_THETA = 500000.0
SCALE = HD ** -0.5
HG_HEADS = 8
HG_DK = 128
HG_DV = 128
HG_CHUNK = 64
MEM_LEN = 256
X_HEADS = 4
X_HD = 128
X_W = X_HEADS * X_HD
X_SCALE = X_HD ** -0.5
N_EXPERTS = 32
TOP_K = 4
D_FF = D_MODEL
SWIGLU_LIMIT = 7.0
SWIGLU_ALPHA = 1.702
MOE_BLOCK = 128
NORM_EPS = 1e-5

NSA_W = NSA_HEADS * HD
HG_KW = HG_HEADS * HG_DK
HG_VW = HG_HEADS * HG_DV
SPLIT_SIZES = (NSA_W, 6 * NSA_KV * HD, NSA_HEADS * 3, HG_KW, HG_KW, HG_VW, HG_VW, D_MODEL, D_MODEL)
D_IN = sum(SPLIT_SIZES)

kernel_name = 'nsa_hgrn2_moe_hybrid_step'


def rmsnorm(x, g):
    xf = x.astype(jnp.float32)
    y = xf * lax.rsqrt(jnp.mean(xf * xf, axis=-1, keepdims=True) + NORM_EPS)
    return (y * g.astype(jnp.float32)).astype(x.dtype)


def rope_partial(x, pos):
    half = ROPE_DIM // 2
    inv = ROPE_THETA ** (-jnp.arange(0, ROPE_DIM, 2, dtype=jnp.float32) / ROPE_DIM)
    ang = pos.astype(jnp.float32)[:, None] * inv[None, :]
    cos = jnp.cos(ang)[:, None, :]
    sin = jnp.sin(ang)[:, None, :]
    xr = x[..., :ROPE_DIM].astype(jnp.float32)
    x1, x2 = xr[..., :half], xr[..., half:]
    rot = jnp.concatenate([x1 * cos - x2 * sin, x2 * cos + x1 * sin], axis=-1).astype(x.dtype)
    return jnp.concatenate([rot, x[..., ROPE_DIM:]], axis=-1)


def masked_softmax(s, mask):
    s = jnp.where(mask, s.astype(jnp.float32), -jnp.inf)
    m = jnp.max(s, axis=-1, keepdims=True)
    m = jnp.where(jnp.isfinite(m), m, 0.0)
    e = jnp.where(mask, jnp.exp(s - m), 0.0)
    d = jnp.sum(e, axis=-1, keepdims=True)
    return e / jnp.where(d > 0, d, 1.0)


def split_cols(p):
    parts = []
    start = 0
    for width in SPLIT_SIZES:
        parts.append(p[..., start:start + width])
        start += width
    return parts


def nsa_compress(rows, pe, w1, w2):
    B, L = rows.shape[:2]
    n_sub = CMP_BLOCK // CMP_STRIDE
    nc = (L - CMP_BLOCK) // CMP_STRIDE + 1
    nch = L // CMP_STRIDE
    ch = rows[:, :nch * CMP_STRIDE].reshape(B, nch, CMP_STRIDE, NSA_KV, HD)
    acc = 0.0
    for r in range(n_sub):
        sl = slice(r * CMP_STRIDE, (r + 1) * CMP_STRIDE)
        pr = jnp.einsum('bnsgd,sdh->bngh', ch + pe[sl][:, None, :], w1[sl])
        acc = acc + pr[:, r:r + nc]
    return jnp.einsum('bngh,hd->bngd', jax.nn.silu(acc), w2)


def cmp_to_slc(p_cmp, ns):
    a = SLC_BLOCK // CMP_STRIDE
    b = CMP_BLOCK // CMP_STRIDE
    nc = p_cmp.shape[-1]
    total = a * ns + a + b
    pad_cfg = [(0, 0)] * (p_cmp.ndim - 1) + [(b - 1, total - (b - 1) - nc)]
    pp = jnp.pad(p_cmp, pad_cfg)
    out = 0.0
    for m in range(a):
        for n in range(b):
            st = a - 1 - m - n + (b - 1)
            out = out + pp[..., st:st + a * ns:a]
    return out


def nsa_cmp_slc(q, q_rot, pos, kc, vc, ks_rows, vs_rows):
    B, T = q.shape[:2]
    L = ks_rows.shape[1]
    nc = kc.shape[1]
    ns = -(-L // SLC_BLOCK)
    nsel = min(SLC_TOP, ns)
    pad = ns * SLC_BLOCK - L

    def to_blocks(r):
        r = jnp.pad(r, ((0, 0), (0, pad), (0, 0), (0, 0)))
        return r.reshape(B, ns, SLC_BLOCK, NSA_KV, HD).transpose(0, 3, 1, 2, 4)

    ks_blk = to_blocks(ks_rows)
    vs_blk = to_blocks(vs_rows)
    cmp_end = jnp.arange(nc) * CMP_STRIDE + CMP_BLOCK - 1
    blk_ids = jnp.arange(ns)
    gather_blocks = jax.vmap(jax.vmap(lambda kb, ix: kb[ix]))

    def one_block(args):
        qr, qo, ps = args
        qb = ps.shape[0]
        s = jnp.einsum('bqghd,bngd->bqghn', qr, kc) * SCALE
        mc = (cmp_end[None, :] <= ps[:, None])[None, :, None, None, :]
        p = masked_softmax(s, mc)
        o_c = jnp.einsum('bqghn,bngd->bqghd', p.astype(vc.dtype), vc)
        imp = cmp_to_slc(jnp.sum(p, axis=3), ns)
        cur = ps // SLC_BLOCK
        vis = blk_ids[None, :] * SLC_BLOCK <= ps[:, None]
        forced = (blk_ids[None, :] == 0) | (blk_ids[None, :] == cur[:, None]) | (blk_ids[None, :] == cur[:, None] - 1)
        score = jnp.where(forced[None, :, None, :], jnp.inf, imp)
        score = jnp.where(vis[None, :, None, :], score, -jnp.inf)
        _, idx = lax.top_k(score, nsel)
        idx_g = idx.transpose(0, 2, 1, 3)
        kg = gather_blocks(ks_blk, idx_g)
        vg = gather_blocks(vs_blk, idx_g)
        tok = idx_g[..., None] * SLC_BLOCK + jnp.arange(SLC_BLOCK)
        ms = (tok <= ps[None, None, :, None, None]).transpose(0, 2, 1, 3, 4)[:, :, :, None]
        ss = jnp.einsum('bqghd,bgqnsd->bqghns', qo, kg) * SCALE
        ps_ = masked_softmax(ss.reshape(B, qb, NSA_KV, NSA_HG, nsel * SLC_BLOCK),
                             ms.reshape(B, qb, NSA_KV, 1, nsel * SLC_BLOCK))
        ps_ = ps_.reshape(B, qb, NSA_KV, NSA_HG, nsel, SLC_BLOCK).astype(vg.dtype)
        o_s = jnp.einsum('bqghns,bgqnsd->bqghd', ps_, vg)
        return o_c, o_s

    qb = math.gcd(T, QBLOCK)
    nqb = T // qb

    def split_q(a):
        return a.reshape(B, nqb, qb, NSA_KV, NSA_HG, HD).transpose(1, 0, 2, 3, 4, 5)

    def merge_q(a):
        return a.transpose(1, 0, 2, 3, 4, 5).reshape(B, T, NSA_HEADS, HD)

    o_c, o_s = lax.map(one_block, (split_q(q), split_q(q_rot), pos.reshape(nqb, qb)))
    return merge_q(o_c), merge_q(o_s)


def nsa_window_banded(q_rot, win_rows, pos):
    B, T = q_rot.shape[:2]
    qb = math.gcd(T, QBLOCK)
    nb = T // qb
    span = qb + WINDOW
    rows = jnp.pad(win_rows, ((0, 0), (WINDOW, 0), (0, 0), (0, 0), (0, 0)))
    idx = jnp.arange(nb)[:, None] * qb + jnp.arange(span)[None, :]
    kv = rows[:, idx]
    kpos = idx - WINDOW
    delta = pos.reshape(nb, qb)[:, :, None] - kpos[:, None, :]
    mask = (delta >= 0) & (delta <= WINDOW) & (kpos[:, None, :] >= 0)
    q = q_rot.reshape(B, nb, qb, NSA_KV, NSA_HG, HD)
    s = jnp.einsum('bnqghd,bnkgd->bnqghk', q, kv[:, :, :, 0]) * SCALE
    p = masked_softmax(s, mask[None, :, :, None, None, :]).astype(kv.dtype)
    o = jnp.einsum('bnqghk,bnkgd->bnqghd', p, kv[:, :, :, 1])
    return o.reshape(B, T, NSA_HEADS, HD)


def nsa_window_buffer(q_rot, win_all, qpos, kpos):
    B, T = q_rot.shape[:2]
    q = q_rot.reshape(B, T, NSA_KV, NSA_HG, HD)
    s = jnp.einsum('bqghd,bkgd->bqghk', q, win_all[:, :, 0]) * SCALE
    delta = qpos[:, None] - kpos[None, :]
    mask = ((delta >= 0) & (delta <= WINDOW))[None, :, None, None, :]
    p = masked_softmax(s, mask).astype(win_all.dtype)
    o = jnp.einsum('bqghk,bkgd->bqghd', p, win_all[:, :, 1])
    return o.reshape(B, T, NSA_HEADS, HD)


def nsa_combine(q, q_rot, pos, cmp_rows, slc_rows, o_win, gate_logits, pe_k, w1_k, w2_k, pe_v, w1_v, w2_v):
    B, T = q.shape[:2]
    kc = nsa_compress(cmp_rows[:, :, 0], pe_k, w1_k, w2_k)
    vc = nsa_compress(cmp_rows[:, :, 1], pe_v, w1_v, w2_v)
    o_c, o_s = nsa_cmp_slc(q, q_rot, pos, kc, vc, slc_rows[:, :, 0], slc_rows[:, :, 1])
    g = jax.nn.sigmoid(gate_logits.astype(jnp.float32)).reshape(B, T, NSA_HEADS, 3).astype(q.dtype)
    o = g[..., 0:1] * o_c + g[..., 1:2] * o_s + g[..., 2:3] * o_win
    return o.reshape(B, T, NSA_W)


def hgrn2_chunked(q, k, v, logf, s0):
    B, T, H, DK = q.shape
    DV = v.shape[-1]
    C = math.gcd(T, HG_CHUNK)
    n = T // C

    def time_major(a):
        return a.reshape(B, n, C, H, a.shape[-1]).transpose(1, 0, 3, 2, 4)

    causal = jnp.arange(C)[:, None] >= jnp.arange(C)[None, :]

    def step(S, inp):
        qc, kc, vc, gc = inp
        b = jnp.cumsum(gc, axis=2)
        inter = jnp.einsum('bhtd,bhde->bhte', qc * jnp.exp(b), S)
        diff = b[:, :, :, None, :] - b[:, :, None, :, :]
        decay = jnp.exp(jnp.where(causal[:, :, None], diff, -jnp.inf))
        A = jnp.einsum('bhtd,bhsd,bhtsd->bhts', qc, kc, decay)
        o = inter + jnp.einsum('bhts,bhse->bhte', A, vc)
        b_last = b[:, :, -1:, :]
        S = jnp.exp(b_last[:, :, 0, :])[..., None] * S + jnp.einsum('bhsd,bhse->bhde', kc * jnp.exp(b_last - b), vc)
        return S, o

    S, o = lax.scan(step, s0, (time_major(q), time_major(k), time_major(v), time_major(logf)))
    return o.transpose(1, 0, 3, 2, 4).reshape(B, T, H, DV), S


def hgrn2_mixer(hq, hf, hv, hg, lb, g_norm, s0):
    B, T = hq.shape[:2]
    f32 = jnp.float32
    q = hq.astype(f32).reshape(B, T, HG_HEADS, HG_DK)
    lbh = lb.reshape(HG_HEADS, HG_DK)
    f = lbh + (1.0 - lbh) * jax.nn.sigmoid(hf.astype(f32).reshape(B, T, HG_HEADS, HG_DK))
    logf = jnp.log(f)
    k = 1.0 - f
    v = hv.astype(f32).reshape(B, T, HG_HEADS, HG_DV)
    o, S = hgrn2_chunked(q, k, v, logf, s0.astype(f32))
    o = rmsnorm(o, g_norm.reshape(HG_HEADS, HG_DV)).reshape(B, T, HG_VW) * jax.nn.silu(hg.astype(f32))
    return o.astype(hq.dtype), S.astype(s0.dtype)


def project_mixers(h, w_in_l, pos):
    B, T = h.shape[:2]
    nq, nkv, ng, hq, hf, hv, hg, ga, gb = split_cols(h @ w_in_l)
    q = nq.reshape(B, T, NSA_HEADS, HD)
    q_rot = rope_partial(q, pos)
    kv6 = nkv.reshape(B, T, 6, NSA_KV, HD)
    cmp_rows = kv6[:, :, 0:2]
    slc_rows = jnp.stack([rope_partial(kv6[:, :, 2], pos), kv6[:, :, 3]], axis=2)
    win_rows = jnp.stack([rope_partial(kv6[:, :, 4], pos), kv6[:, :, 5]], axis=2)
    return q, q_rot, cmp_rows, slc_rows, win_rows, ng, (hq, hf, hv, hg), ga, gb


def merge_branches(o_nsa, o_hg, ga, gb, wa, wb, wo):
    f32 = jnp.float32
    a = (o_nsa @ wa).astype(f32)
    b = (o_hg @ wb).astype(f32)
    m = jax.nn.sigmoid(ga.astype(f32)) * a + jax.nn.sigmoid(gb.astype(f32)) * b
    return m.astype(o_nsa.dtype) @ wo


def memory_kv(mem, wk, wv):
    B, M = mem.shape[:2]
    k = (mem @ wk).reshape(B, M, X_HEADS, X_HD)
    v = (mem @ wv).reshape(B, M, X_HEADS, X_HD)
    return jnp.stack([k, v], axis=2)


def cross_attn(h, kv, wq, wo):
    B, T = h.shape[:2]
    q = (h @ wq).reshape(B, T, X_HEADS, X_HD)
    s = jnp.einsum('bthd,bmhd->bhtm', q, kv[:, :, 0]).astype(jnp.float32) * X_SCALE
    p = jax.nn.softmax(s, axis=-1).astype(h.dtype)
    o = jnp.einsum('bhtm,bmhd->bthd', p, kv[:, :, 1]).reshape(B, T, X_W)
    return o @ wo


def moe_ffn(h, w_r, b_r, w_gu, b_gu, w_dn, b_dn):
    B, T, D = h.shape
    n_tok = B * T
    x = h.reshape(n_tok, D)
    logits = (x @ w_r + b_r).astype(jnp.float32)
    top_val, top_idx = lax.top_k(logits, TOP_K)
    gates = jax.nn.softmax(top_val, axis=-1)
    n_assign = n_tok * TOP_K
    e_flat = top_idx.reshape(-1)
    tok_flat = jnp.arange(n_assign, dtype=jnp.int32) // TOP_K
    g_flat = gates.reshape(-1)
    order = jnp.argsort(e_flat)
    e_sorted = e_flat[order]
    counts = jnp.zeros((N_EXPERTS,), jnp.int32).at[e_flat].add(1)
    padded = (counts + MOE_BLOCK - 1) // MOE_BLOCK * MOE_BLOCK
    start = jnp.cumsum(counts) - counts
    pend = jnp.cumsum(padded)
    pstart = pend - padded
    dest = pstart[e_sorted] + jnp.arange(n_assign, dtype=jnp.int32) - start[e_sorted]
    n_blk = -(-n_assign // MOE_BLOCK) + N_EXPERTS
    n_rows = n_blk * MOE_BLOCK
    row_tok = jnp.full((n_rows,), n_tok, jnp.int32).at[dest].set(tok_flat[order])
    row_gate = jnp.zeros((n_rows,), jnp.float32).at[dest].set(g_flat[order])
    blk_e = jnp.minimum(jnp.sum(jnp.arange(n_blk)[:, None] * MOE_BLOCK >= pend[None, :], axis=1), N_EXPERTS - 1)
    x_pad = jnp.concatenate([x, jnp.zeros((1, D), x.dtype)], axis=0)
    xb = x_pad[row_tok].reshape(n_blk, MOE_BLOCK, D)

    def expert_block(args):
        xblk, e = args
        gu = xblk @ w_gu[e] + b_gu[e]
        gate = jnp.minimum(gu[:, 0::2], SWIGLU_LIMIT)
        up = jnp.clip(gu[:, 1::2], -SWIGLU_LIMIT, SWIGLU_LIMIT)
        act = (up + 1.0) * gate * jax.nn.sigmoid(gate * SWIGLU_ALPHA)
        return act @ w_dn[e] + b_dn[e]

    yb = lax.map(expert_block, (xb, blk_e)).reshape(n_rows, D)
    y = jnp.zeros((n_tok + 1, D), yb.dtype).at[row_tok].add(yb * row_gate[:, None].astype(yb.dtype))
    return y[:n_tok].reshape(B, T, D).astype(h.dtype)


def paged_rows(pool, layer, page_table):
    rows = pool[layer, page_table]
    db, n_pages = page_table.shape
    return rows.reshape((db, n_pages * PAGE_SIZE) + rows.shape[3:])


def setup_inputs(seed: int = 0) -> dict:
    key = jax.random.key(seed)
    keys = iter(jax.random.split(key, 48))
    f32 = jnp.float32

    def nrm(shape, scale=1.0):
        return jax.random.normal(next(keys), shape, f32) * scale

    def gain(shape):
        return 1.0 + nrm(shape, 0.02)

    n_pages = PAST_LEN // PAGE_SIZE
    n_used = DEC_BATCH * n_pages
    n_pool = n_used + max(1, n_used // 4)
    win_len = min(WINDOW, PAST_LEN)
    x_prompt = nrm((BATCH, SEQ, D_MODEL))
    x_sample = nrm((DEC_BATCH, DEC_SEQ, D_MODEL))
    cache_cmp_kv = nrm((DEPTH, n_pool, PAGE_SIZE, 2, NSA_KV, HD))
    cache_slc_kv = nrm((DEPTH, n_pool, PAGE_SIZE, 2, NSA_KV, HD))
    cache_win_kv = nrm((DEPTH, DEC_BATCH, win_len, 2, NSA_KV, HD))
    state_hgrn = nrm((DEPTH, DEC_BATCH, HG_HEADS, HG_DK, HG_DV), 0.3)
    cache_mem_kv = nrm((DEPTH, DEC_BATCH, MEM_LEN, 2, X_HEADS, X_HD))
    perm = jax.random.permutation(next(keys), n_pool)
    page_table = perm[:n_used].reshape(DEC_BATCH, n_pages).astype(jnp.int32)
    mem_prompt = nrm((BATCH, MEM_LEN, D_MODEL))
    return {
        'x_prompt': x_prompt,
        'x_sample': x_sample,
        'cache_cmp_kv': cache_cmp_kv,
        'cache_slc_kv': cache_slc_kv,
        'cache_win_kv': cache_win_kv,
        'state_hgrn': state_hgrn,
        'cache_mem_kv': cache_mem_kv,
        'page_table': page_table,
        'mem_prompt': mem_prompt,
        'norm_mix': gain((DEPTH, D_MODEL)),
        'w_in': nrm((DEPTH, D_MODEL, D_IN), D_MODEL ** -0.5),
        'cmp_pe_k': nrm((DEPTH, CMP_BLOCK, HD), 0.1),
        'cmp_w1_k': nrm((DEPTH, CMP_BLOCK, HD, CMP_HID), (CMP_BLOCK * HD) ** -0.5),
        'cmp_w2_k': nrm((DEPTH, CMP_HID, HD), CMP_HID ** -0.5),
        'cmp_pe_v': nrm((DEPTH, CMP_BLOCK, HD), 0.1),
        'cmp_w1_v': nrm((DEPTH, CMP_BLOCK, HD, CMP_HID), (CMP_BLOCK * HD) ** -0.5),
        'cmp_w2_v': nrm((DEPTH, CMP_HID, HD), CMP_HID ** -0.5),
        'lb_logits': nrm((DEPTH + 1, HG_KW), 0.5),
        'hg_norm': gain((DEPTH, HG_VW)),
        'w_branch_a': nrm((DEPTH, NSA_W, D_MODEL), NSA_W ** -0.5),
        'w_branch_b': nrm((DEPTH, HG_VW, D_MODEL), HG_VW ** -0.5),
        'w_out': nrm((DEPTH, D_MODEL, D_MODEL), D_MODEL ** -0.5),
        'norm_x': gain((DEPTH, D_MODEL)),
        'wq_x': nrm((DEPTH, D_MODEL, X_W), D_MODEL ** -0.5),
        'wk_x': nrm((DEPTH, D_MODEL, X_W), D_MODEL ** -0.5),
        'wv_x': nrm((DEPTH, D_MODEL, X_W), D_MODEL ** -0.5),
        'wo_x': nrm((DEPTH, X_W, D_MODEL), X_W ** -0.5),
        'norm_ffn': gain((DEPTH, D_MODEL)),
        'w_router': nrm((DEPTH, D_MODEL, N_EXPERTS), D_MODEL ** -0.5),
        'b_router': nrm((DEPTH, N_EXPERTS), 0.01),
        'w_gu': nrm((DEPTH, N_EXPERTS, D_MODEL, 2 * D_FF), D_MODEL ** -0.5),
        'b_gu': nrm((DEPTH, N_EXPERTS, 2 * D_FF), 0.01),
        'w_dn': nrm((DEPTH, N_EXPERTS, D_FF, D_MODEL), D_FF ** -0.5),
        'b_dn': nrm((DEPTH, N_EXPERTS, D_MODEL), 0.01),
        'norm_final': gain((D_MODEL,)),
    }


def reference(x_prompt, x_sample, cache_cmp_kv, cache_slc_kv, cache_win_kv, state_hgrn, cache_mem_kv, page_table,
              mem_prompt, norm_mix, w_in, cmp_pe_k, cmp_w1_k, cmp_w2_k, cmp_pe_v, cmp_w1_v, cmp_w2_v, lb_logits,
              hg_norm, w_branch_a, w_branch_b, w_out, norm_x, wq_x, wk_x, wv_x, wo_x, norm_ffn, w_router, b_router,
              w_gu, b_gu, w_dn, b_dn, norm_final):
    seq = x_prompt.shape[1]
    dec_seq = x_sample.shape[1]
    past = page_table.shape[1] * PAGE_SIZE
    win_buf_len = cache_win_kv.shape[2]
    pos_p = jnp.arange(seq, dtype=jnp.int32)
    pos_s = past + jnp.arange(dec_seq, dtype=jnp.int32)
    kpos_w = jnp.arange(past - win_buf_len, past + dec_seq, dtype=jnp.int32)
    lb_all = jnp.cumsum(jax.nn.softmax(lb_logits.astype(jnp.float32), axis=0), axis=0)
    xp, xs = x_prompt, x_sample
    p_cmp, p_slc, p_win, p_hg, p_mem = [], [], [], [], []
    s_cmp, s_slc, s_win, s_hg = [], [], [], []
    for l in range(DEPTH):
        cmp_par = (cmp_pe_k[l], cmp_w1_k[l], cmp_w2_k[l], cmp_pe_v[l], cmp_w1_v[l], cmp_w2_v[l])
        h = rmsnorm(xp, norm_mix[l])
        q, q_rot, cmp_rows, slc_rows, win_rows, g_nsa, hg_in, ga, gb = project_mixers(h, w_in[l], pos_p)
        o_win = nsa_window_banded(q_rot, win_rows, pos_p)
        o_nsa = nsa_combine(q, q_rot, pos_p, cmp_rows, slc_rows, o_win, g_nsa, *cmp_par)
        s0 = jnp.zeros((xp.shape[0], HG_HEADS, HG_DK, HG_DV), xp.dtype)
        o_hg, s_final = hgrn2_mixer(*hg_in, lb_all[l], hg_norm[l], s0)
        xp = xp + merge_branches(o_nsa, o_hg, ga, gb, w_branch_a[l], w_branch_b[l], w_out[l])
        mkv = memory_kv(mem_prompt, wk_x[l], wv_x[l])
        xp = xp + cross_attn(rmsnorm(xp, norm_x[l]), mkv, wq_x[l], wo_x[l])
        xp = xp + moe_ffn(rmsnorm(xp, norm_ffn[l]), w_router[l], b_router[l], w_gu[l], b_gu[l], w_dn[l], b_dn[l])
        p_cmp.append(cmp_rows)
        p_slc.append(slc_rows)
        p_win.append(win_rows[:, seq - min(WINDOW, seq):])
        p_hg.append(s_final)
        p_mem.append(mkv)
        h = rmsnorm(xs, norm_mix[l])
        q, q_rot, cmp_new, slc_new, win_new, g_nsa, hg_in, ga, gb = project_mixers(h, w_in[l], pos_s)
        cmp_all = jnp.concatenate([paged_rows(cache_cmp_kv, l, page_table), cmp_new], axis=1)
        slc_all = jnp.concatenate([paged_rows(cache_slc_kv, l, page_table), slc_new], axis=1)
        win_all = jnp.concatenate([cache_win_kv[l], win_new], axis=1)
        o_win = nsa_window_buffer(q_rot, win_all, pos_s, kpos_w)
        o_nsa = nsa_combine(q, q_rot, pos_s, cmp_all, slc_all, o_win, g_nsa, *cmp_par)
        o_hg, s_new = hgrn2_mixer(*hg_in, lb_all[l], hg_norm[l], state_hgrn[l])
        xs = xs + merge_branches(o_nsa, o_hg, ga, gb, w_branch_a[l], w_branch_b[l], w_out[l])
        xs = xs + cross_attn(rmsnorm(xs, norm_x[l]), cache_mem_kv[l], wq_x[l], wo_x[l])
        xs = xs + moe_ffn(rmsnorm(xs, norm_ffn[l]), w_router[l], b_router[l], w_gu[l], b_gu[l], w_dn[l], b_dn[l])
        s_cmp.append(cmp_new)
        s_slc.append(slc_new)
        s_win.append(win_all[:, win_all.shape[1] - win_buf_len:])
        s_hg.append(s_new)
    y_prompt = rmsnorm(xp, norm_final)
    y_sample = rmsnorm(xs, norm_final)
    return (y_prompt, y_sample,
            jnp.stack(p_cmp), jnp.stack(p_slc), jnp.stack(p_win), jnp.stack(p_hg), jnp.stack(p_mem),
            jnp.stack(s_cmp), jnp.stack(s_slc), jnp.stack(s_win), jnp.stack(s_hg))
```

```python
import functools
import math

import jax
import jax.numpy as jnp
from jax import lax
from jax.experimental import pallas as pl
from jax.experimental.pallas import tpu as pltpu

F32 = jnp.float32
BF16 = jnp.bfloat16
HI = lax.Precision.HIGHEST

D_MODEL = 2048
PAGE_SIZE = 128
NSA_HEADS = 8
NSA_KV = 2
NSA_HG = NSA_HEADS // NSA_KV
HD = 128
CMP_BLOCK = 32
CMP_STRIDE = 16
SLC_BLOCK = 64
SLC_TOP = 16
WINDOW = 512
QBLOCK = 128
ROPE_DIM = HD // 4
ROPE_THETA = 500000.0
SCALE = HD ** -0.5
HG_HEADS = 8
HG_DK = 128
HG_DV = 128
HG_CHUNK = 64
X_HEADS = 4
X_HD = 128
X_W = X_HEADS * X_HD
X_SCALE = X_HD ** -0.5
N_EXPERTS = 32
TOP_K = 4
D_FF = D_MODEL
SWIGLU_LIMIT = 7.0
SWIGLU_ALPHA = 1.702
NORM_EPS = 1e-5
NSA_W = NSA_HEADS * HD
HG_KW = HG_HEADS * HG_DK
HG_VW = HG_HEADS * HG_DV

LANES = 128
MXU_N = 256
VMEM_LIMIT = 52 * 1024 * 1024

COL_GA = 0
COL_GB = COL_GA + D_MODEL
COL_Q = COL_GB + D_MODEL
COL_KV = COL_Q + NSA_W
COL_HQ = COL_KV + 6 * NSA_KV * HD
COL_HF = COL_HQ + HG_KW
COL_HV = COL_HF + HG_KW
COL_HGATE = COL_HV + HG_VW
COL_NG = COL_HGATE + HG_VW
N_GATE = NSA_HEADS * 3
PROJ_TN = 512
D_IN_PAD = -(-(COL_NG + N_GATE) // PROJ_TN) * PROJ_TN

MOE_ROWS = 256
MOE_TN = 1024


def _params(sem):
    return pltpu.CompilerParams(dimension_semantics=sem, vmem_limit_bytes=VMEM_LIMIT)


def _rms(x, g):
    return x * lax.rsqrt(jnp.mean(x * x, axis=-1, keepdims=True) + NORM_EPS) * g


def _dot(a, b, hp):
    if hp:
        return jnp.dot(a.astype(F32), b.astype(F32), precision=HI, preferred_element_type=F32)
    return jnp.dot(a.astype(BF16), b.astype(BF16), preferred_element_type=F32)


def _norm_matmul_kernel(x_ref, g_ref, w_ref, o_ref, h_ref, *, normalize, hp):
    @pl.when(pl.program_id(1) == 0)
    def _():
        x = x_ref[...]
        if normalize:
            x = _rms(x, g_ref[...])
        h_ref[...] = x.astype(h_ref.dtype)

    o_ref[...] = _dot(h_ref[...], w_ref[...], hp)


def _norm_matmul(x, g, w, *, normalize, hp, tm, tn):
    m, k = x.shape
    n = w.shape[1]
    return pl.pallas_call(
        functools.partial(_norm_matmul_kernel, normalize=normalize, hp=hp),
        out_shape=jax.ShapeDtypeStruct((m, n), F32),
        grid=(m // tm, n // tn),
        in_specs=[pl.BlockSpec((tm, k), lambda i, j: (i, 0)),
                  pl.BlockSpec((1, k), lambda i, j: (0, 0)),
                  pl.BlockSpec((k, tn), lambda i, j: (0, j))],
        out_specs=pl.BlockSpec((tm, tn), lambda i, j: (i, j)),
        scratch_shapes=[pltpu.VMEM((tm, k), F32 if hp else BF16)],
        compiler_params=_params(("parallel", "arbitrary")),
    )(x, g.reshape(1, k), w)


def _merge_kernel(x_ref, on_ref, oh_ref, ga_ref, gb_ref, wa_ref, wb_ref, wo_ref, o_ref, *, hp):
    a = _dot(on_ref[...], wa_ref[...], hp)
    b = _dot(oh_ref[...], wb_ref[...], hp)
    m = jax.nn.sigmoid(ga_ref[...]) * a + jax.nn.sigmoid(gb_ref[...]) * b
    o_ref[...] = x_ref[...] + _dot(m, wo_ref[...], hp)


def _merge(x, o_nsa, o_hg, proj, wa, wb, wo, *, hp, tm):
    m = x.shape[0]
    d = D_MODEL
    const = lambda i: (0, 0)
    return pl.pallas_call(
        functools.partial(_merge_kernel, hp=hp),
        out_shape=jax.ShapeDtypeStruct((m, d), F32),
        grid=(m // tm,),
        in_specs=[pl.BlockSpec((tm, d), lambda i: (i, 0)),
                  pl.BlockSpec((tm, NSA_W), lambda i: (i, 0)),
                  pl.BlockSpec((tm, HG_VW), lambda i: (i, 0)),
                  pl.BlockSpec((tm, d), lambda i: (i, COL_GA // d)),
                  pl.BlockSpec((tm, d), lambda i: (i, COL_GB // d)),
                  pl.BlockSpec((NSA_W, d), const, pipeline_mode=pl.Buffered(1)),
                  pl.BlockSpec((HG_VW, d), const, pipeline_mode=pl.Buffered(1)),
                  pl.BlockSpec((d, d), const, pipeline_mode=pl.Buffered(1))],
        out_specs=pl.BlockSpec((tm, d), lambda i: (i, 0)),
        compiler_params=_params(("parallel",)),
    )(x, o_nsa, o_hg, proj, proj, wa, wb, wo)


def _cross_kernel(x_ref, g_ref, k_ref, v_ref, wq_ref, wo_ref, o_ref):
    x = x_ref[0]
    h = _rms(x, g_ref[...])
    q = _dot(h, wq_ref[...], False)
    k = k_ref[0].astype(BF16)
    v = v_ref[0].astype(BF16)
    outs = []
    for hd in range(X_HEADS):
        sl = slice(hd * X_HD, (hd + 1) * X_HD)
        s = lax.dot_general(q[:, sl].astype(BF16), k[:, sl], (((1,), (1,)), ((), ())),
                            preferred_element_type=F32) * X_SCALE
        s = s - jnp.max(s, axis=-1, keepdims=True)
        e = jnp.exp(s)
        p = e / jnp.sum(e, axis=-1, keepdims=True)
        outs.append(jnp.dot(p.astype(BF16), v[:, sl], preferred_element_type=F32))
    o = jnp.concatenate(outs, axis=-1)
    o_ref[0] = x + _dot(o, wo_ref[...], False)


def _cross(x, g, mkv, wq, wo, *, tm):
    b, t, d = x.shape
    mlen = mkv.shape[1]
    const = lambda bi, i: (0, 0)
    return pl.pallas_call(
        _cross_kernel,
        out_shape=jax.ShapeDtypeStruct((b, t, d), F32),
        grid=(b, t // tm),
        in_specs=[pl.BlockSpec((1, tm, d), lambda bi, i: (bi, i, 0)),
                  pl.BlockSpec((1, d), const),
                  pl.BlockSpec((1, mlen, X_W), lambda bi, i: (bi, 0, 0)),
                  pl.BlockSpec((1, mlen, X_W), lambda bi, i: (bi, 0, 1)),
                  pl.BlockSpec((d, X_W), const),
                  pl.BlockSpec((X_W, d), const)],
        out_specs=pl.BlockSpec((1, tm, d), lambda bi, i: (bi, i, 0)),
        compiler_params=_params(("parallel", "parallel")),
    )(x, g.reshape(1, d), mkv, mkv, wq, wo)


def _router_kernel(x_ref, g_ref, w_ref, b_ref, h_ref, l_ref):
    h = _rms(x_ref[...], g_ref[...])
    h_ref[...] = h.astype(BF16)
    l_ref[...] = _dot(h, w_ref[...], True) + b_ref[...]


def _router(x, g, w_r_pad, b_r_pad, *, tm):
    m, d = x.shape
    const = lambda i: (0, 0)
    return pl.pallas_call(
        _router_kernel,
        out_shape=(jax.ShapeDtypeStruct((m, d), BF16), jax.ShapeDtypeStruct((m, LANES), F32)),
        grid=(m // tm,),
        in_specs=[pl.BlockSpec((tm, d), lambda i: (i, 0)),
                  pl.BlockSpec((1, d), const),
                  pl.BlockSpec((d, LANES), const),
                  pl.BlockSpec((1, LANES), const)],
        out_specs=(pl.BlockSpec((tm, d), lambda i: (i, 0)), pl.BlockSpec((tm, LANES), lambda i: (i, 0))),
        compiler_params=_params(("parallel",)),
    )(x, g.reshape(1, d), w_r_pad, b_r_pad)


def _new_weight_tile(blk_e_ref, i):
    prev = blk_e_ref[jnp.maximum(i - 1, 0)]
    return (i == 0) | (blk_e_ref[i] != prev)


def _moe_gu_kernel(blk_e_ref, nused_ref, x_ref, w_ref, b_ref, sel_ref, o_ref, wbf_ref):
    i = pl.program_id(1)
    tn = w_ref.shape[2]

    @pl.when(_new_weight_tile(blk_e_ref, i))
    def _():
        wbf_ref[...] = w_ref[0].astype(BF16)

    @pl.when(i < nused_ref[0])
    def _():
        gu = jnp.dot(x_ref[...], wbf_ref[...], preferred_element_type=F32) + b_ref[0]
        gate = jnp.minimum(gu, SWIGLU_LIMIT)
        up = jnp.clip(gu, -SWIGLU_LIMIT, SWIGLU_LIMIT)
        up_next = pltpu.roll(up, tn - 1, axis=1)
        act = (up_next + 1.0) * gate * jax.nn.sigmoid(gate * SWIGLU_ALPHA)
        lane = lax.broadcasted_iota(jnp.int32, act.shape, 1)
        act = jnp.where(lane % 2 == 0, act, 0.0).astype(BF16)
        for c in range(tn // MXU_N):
            o_ref[:, c * LANES:(c + 1) * LANES] = jnp.dot(
                act[:, c * MXU_N:(c + 1) * MXU_N], sel_ref[...], preferred_element_type=F32).astype(BF16)

    @pl.when(i >= nused_ref[0])
    def _():
        o_ref[...] = jnp.zeros_like(o_ref)


def _moe_dn_kernel(blk_e_ref, nused_ref, a_ref, w_ref, b_ref, o_ref, wbf_ref):
    i = pl.program_id(1)

    @pl.when(_new_weight_tile(blk_e_ref, i))
    def _():
        wbf_ref[...] = w_ref[0].astype(BF16)

    @pl.when(i < nused_ref[0])
    def _():
        o_ref[...] = jnp.dot(a_ref[...], wbf_ref[...], preferred_element_type=F32) + b_ref[0]

    @pl.when(i >= nused_ref[0])
    def _():
        o_ref[...] = jnp.zeros_like(o_ref)


def _moe_experts(xb, blk_e, nused, w_gu, b_gu, w_dn, b_dn):
    n_rows, d = xb.shape
    n_blk = n_rows // MOE_ROWS
    tn = MOE_TN
    sel = (jnp.arange(MXU_N)[:, None] == 2 * jnp.arange(LANES)[None, :]).astype(BF16)
    act = pl.pallas_call(
        _moe_gu_kernel,
        out_shape=jax.ShapeDtypeStruct((n_rows, D_FF), BF16),
        grid_spec=pltpu.PrefetchScalarGridSpec(
            num_scalar_prefetch=2,
            grid=(2 * D_FF // tn, n_blk),
            in_specs=[pl.BlockSpec((MOE_ROWS, d), lambda j, i, be, nu: (i, 0)),
                      pl.BlockSpec((1, d, tn), lambda j, i, be, nu: (be[i], 0, j)),
                      pl.BlockSpec((1, 1, tn), lambda j, i, be, nu: (be[i], 0, j)),
                      pl.BlockSpec((MXU_N, LANES), lambda j, i, be, nu: (0, 0))],
            out_specs=pl.BlockSpec((MOE_ROWS, tn // 2), lambda j, i, be, nu: (i, j)),
            scratch_shapes=[pltpu.VMEM((d, tn), BF16)]),
        compiler_params=_params(("arbitrary", "arbitrary")),
    )(blk_e, nused, xb, w_gu, b_gu.reshape(N_EXPERTS, 1, 2 * D_FF), sel)
    return pl.pallas_call(
        _moe_dn_kernel,
        out_shape=jax.ShapeDtypeStruct((n_rows, d), F32),
        grid_spec=pltpu.PrefetchScalarGridSpec(
            num_scalar_prefetch=2,
            grid=(d // tn, n_blk),
            in_specs=[pl.BlockSpec((MOE_ROWS, D_FF), lambda j, i, be, nu: (i, 0)),
                      pl.BlockSpec((1, D_FF, tn), lambda j, i, be, nu: (be[i], 0, j)),
                      pl.BlockSpec((1, 1, tn), lambda j, i, be, nu: (be[i], 0, j))],
            out_specs=pl.BlockSpec((MOE_ROWS, tn), lambda j, i, be, nu: (i, j)),
            scratch_shapes=[pltpu.VMEM((D_FF, tn), BF16)]),
        compiler_params=_params(("arbitrary", "arbitrary")),
    )(blk_e, nused, act, w_dn, b_dn.reshape(N_EXPERTS, 1, d))


def _combine_kernel(x_ref, y_ref, gt_ref, g_ref, o_ref):
    acc = x_ref[...]
    gt = gt_ref[...]
    for k in range(TOP_K):
        acc = acc + y_ref[k] * gt[:, k:k + 1]
    o_ref[...] = _rms(acc, g_ref[...])


def _combine(x, yg, gates, g, *, tm, row0):
    m, d = x.shape
    off = row0 // tm
    return pl.pallas_call(
        _combine_kernel,
        out_shape=jax.ShapeDtypeStruct((m, d), F32),
        grid=(m // tm,),
        in_specs=[pl.BlockSpec((tm, d), lambda i: (i, 0)),
                  pl.BlockSpec((TOP_K, tm, d), lambda i: (0, i + off, 0)),
                  pl.BlockSpec((tm, TOP_K), lambda i: (i + off, 0)),
                  pl.BlockSpec((1, d), lambda i: (0, 0))],
        out_specs=pl.BlockSpec((tm, d), lambda i: (i, 0)),
        compiler_params=_params(("parallel",)),
    )(x, yg, gates, g.reshape(1, d))


def _rope_partial(x, pos):
    half = ROPE_DIM // 2
    inv = ROPE_THETA ** (-jnp.arange(0, ROPE_DIM, 2, dtype=F32) / ROPE_DIM)
    ang = pos.astype(F32)[:, None] * inv[None, :]
    cos = jnp.cos(ang)[:, None, :]
    sin = jnp.sin(ang)[:, None, :]
    xr = x[..., :ROPE_DIM]
    x1, x2 = xr[..., :half], xr[..., half:]
    rot = jnp.concatenate([x1 * cos - x2 * sin, x2 * cos + x1 * sin], axis=-1)
    return jnp.concatenate([rot, x[..., ROPE_DIM:]], axis=-1)


def _masked_softmax(s, mask):
    s = jnp.where(mask, s, -jnp.inf)
    m = jnp.max(s, axis=-1, keepdims=True)
    m = jnp.where(jnp.isfinite(m), m, 0.0)
    e = jnp.where(mask, jnp.exp(s - m), 0.0)
    d = jnp.sum(e, axis=-1, keepdims=True)
    return e / jnp.where(d > 0, d, 1.0)


def _nsa_compress(rows, pe, w1, w2, prec):
    b, l = rows.shape[:2]
    n_sub = CMP_BLOCK // CMP_STRIDE
    nc = (l - CMP_BLOCK) // CMP_STRIDE + 1
    nch = l // CMP_STRIDE
    ch = rows[:, :nch * CMP_STRIDE].reshape(b, nch, CMP_STRIDE, NSA_KV, HD)
    acc = 0.0
    for r in range(n_sub):
        sl = slice(r * CMP_STRIDE, (r + 1) * CMP_STRIDE)
        pr = jnp.einsum('bnsgd,sdh->bngh', ch + pe[sl][:, None, :], w1[sl], precision=prec)
        acc = acc + pr[:, r:r + nc]
    return jnp.einsum('bngh,hd->bngd', jax.nn.silu(acc), w2, precision=prec)


def _cmp_to_slc(p_cmp, ns):
    a = SLC_BLOCK // CMP_STRIDE
    b = CMP_BLOCK // CMP_STRIDE
    nc = p_cmp.shape[-1]
    total = a * ns + a + b
    pad_cfg = [(0, 0)] * (p_cmp.ndim - 1) + [(b - 1, total - (b - 1) - nc)]
    pp = jnp.pad(p_cmp, pad_cfg)
    out = 0.0
    for m in range(a):
        for n in range(b):
            st = a - 1 - m - n + (b - 1)
            out = out + pp[..., st:st + a * ns:a]
    return out


def _nsa_cmp_slc(q, q_rot, pos, kc, vc, ks_rows, vs_rows, prec):
    b, t = q.shape[:2]
    l = ks_rows.shape[1]
    nc = kc.shape[1]
    ns = -(-l // SLC_BLOCK)
    nsel = min(SLC_TOP, ns)
    cmp_end = jnp.arange(nc) * CMP_STRIDE + CMP_BLOCK - 1
    blk_ids = jnp.arange(ns)
    key_pos = jnp.arange(l)
    key_blk = key_pos // SLC_BLOCK

    def one_block(args):
        qr, qo, ps = args
        s = jnp.einsum('bqghd,bngd->bqghn', qr, kc, precision=prec) * SCALE
        mc = (cmp_end[None, :] <= ps[:, None])[None, :, None, None, :]
        p = _masked_softmax(s, mc)
        o_c = jnp.einsum('bqghn,bngd->bqghd', p, vc, precision=prec)
        imp = _cmp_to_slc(jnp.sum(p, axis=3), ns)
        cur = ps // SLC_BLOCK
        vis = blk_ids[None, :] * SLC_BLOCK <= ps[:, None]
        forced = (blk_ids[None, :] == 0) | (blk_ids[None, :] == cur[:, None]) | (blk_ids[None, :] == cur[:, None] - 1)
        score = jnp.where(forced[None, :, None, :], jnp.inf, imp)
        score = jnp.where(vis[None, :, None, :], score, -jnp.inf)
        _, idx = lax.top_k(score, nsel)
        sel = jnp.any(idx[..., None] == blk_ids, axis=-2)
        ms = sel[..., key_blk] & (key_pos[None, :] <= ps[:, None])[None, :, None, :]
        ss = jnp.einsum('bqghd,bkgd->bqghk', qo, ks_rows, precision=prec) * SCALE
        pss = _masked_softmax(ss, ms[:, :, :, None, :])
        o_s = jnp.einsum('bqghk,bkgd->bqghd', pss, vs_rows, precision=prec)
        return o_c, o_s

    qb = math.gcd(t, QBLOCK)
    nqb = t // qb

    def split_q(a):
        return a.reshape(b, nqb, qb, NSA_KV, NSA_HG, HD).transpose(1, 0, 2, 3, 4, 5)

    def merge_q(a):
        return a.transpose(1, 0, 2, 3, 4, 5).reshape(b, t, NSA_HEADS, HD)

    o_c, o_s = lax.map(one_block, (split_q(q), split_q(q_rot), pos.reshape(nqb, qb)))
    return merge_q(o_c), merge_q(o_s)


def _nsa_window_banded(q_rot, win_rows, pos):
    b, t = q_rot.shape[:2]
    qb = math.gcd(t, QBLOCK)
    nb = t // qb
    span = qb + WINDOW
    rows = jnp.pad(win_rows, ((0, 0), (WINDOW, 0), (0, 0), (0, 0), (0, 0)))
    idx = jnp.arange(nb)[:, None] * qb + jnp.arange(span)[None, :]
    kv = rows[:, idx]
    kpos = idx - WINDOW
    delta = pos.reshape(nb, qb)[:, :, None] - kpos[:, None, :]
    mask = (delta >= 0) & (delta <= WINDOW) & (kpos[:, None, :] >= 0)
    q = q_rot.reshape(b, nb, qb, NSA_KV, NSA_HG, HD)
    s = jnp.einsum('bnqghd,bnkgd->bnqghk', q, kv[:, :, :, 0]) * SCALE
    p = _masked_softmax(s, mask[None, :, :, None, None, :])
    o = jnp.einsum('bnqghk,bnkgd->bnqghd', p, kv[:, :, :, 1])
    return o.reshape(b, t, NSA_HEADS, HD)


def _nsa_window_buffer(q_rot, win_all, qpos, kpos, prec):
    b, t = q_rot.shape[:2]
    q = q_rot.reshape(b, t, NSA_KV, NSA_HG, HD)
    s = jnp.einsum('bqghd,bkgd->bqghk', q, win_all[:, :, 0], precision=prec) * SCALE
    delta = qpos[:, None] - kpos[None, :]
    mask = ((delta >= 0) & (delta <= WINDOW))[None, :, None, None, :]
    p = _masked_softmax(s, mask)
    o = jnp.einsum('bqghk,bkgd->bqghd', p, win_all[:, :, 1], precision=prec)
    return o.reshape(b, t, NSA_HEADS, HD)


def _nsa_combine(q, q_rot, pos, cmp_rows, slc_rows, o_win, gate_logits, cmp_par, prec):
    b, t = q.shape[:2]
    pe_k, w1_k, w2_k, pe_v, w1_v, w2_v = cmp_par
    kc = _nsa_compress(cmp_rows[:, :, 0], pe_k, w1_k, w2_k, None)
    vc = _nsa_compress(cmp_rows[:, :, 1], pe_v, w1_v, w2_v, None)
    o_c, o_s = _nsa_cmp_slc(q, q_rot, pos, kc, vc, slc_rows[:, :, 0], slc_rows[:, :, 1], prec)
    g = jax.nn.sigmoid(gate_logits).reshape(b, t, NSA_HEADS, 3)
    o = g[..., 0:1] * o_c + g[..., 1:2] * o_s + g[..., 2:3] * o_win
    return o.reshape(b, t, NSA_W)


def _hgrn2_chunked(q, k, v, logf, s0, prec):
    b, t, h, dk = q.shape
    dv = v.shape[-1]
    c = math.gcd(t, HG_CHUNK)
    n = t // c

    def time_major(a):
        return a.reshape(b, n, c, h, a.shape[-1]).transpose(1, 0, 3, 2, 4)

    causal = jnp.arange(c)[:, None] >= jnp.arange(c)[None, :]

    def step(S, inp):
        qc, kc, vc, gc = inp
        bb = jnp.cumsum(gc, axis=2)
        inter = jnp.einsum('bhtd,bhde->bhte', qc * jnp.exp(bb), S, precision=prec)
        diff = bb[:, :, :, None, :] - bb[:, :, None, :, :]
        decay = jnp.exp(jnp.where(causal[:, :, None], diff, -jnp.inf))
        A = jnp.einsum('bhtd,bhsd,bhtsd->bhts', qc, kc, decay, precision=prec)
        o = inter + jnp.einsum('bhts,bhse->bhte', A, vc, precision=prec)
        b_last = bb[:, :, -1:, :]
        S = jnp.exp(b_last[:, :, 0, :])[..., None] * S + jnp.einsum(
            'bhsd,bhse->bhde', kc * jnp.exp(b_last - bb), vc, precision=prec)
        return S, o

    S, o = lax.scan(step, s0, (time_major(q), time_major(k), time_major(v), time_major(logf)))
    return o.transpose(1, 0, 3, 2, 4).reshape(b, t, h, dv), S


def _hgrn2_mixer(hq, hf, hv, hg, lb, g_norm, s0, prec):
    b, t = hq.shape[:2]
    q = hq.reshape(b, t, HG_HEADS, HG_DK)
    lbh = lb.reshape(HG_HEADS, HG_DK)
    f = lbh + (1.0 - lbh) * jax.nn.sigmoid(hf.reshape(b, t, HG_HEADS, HG_DK))
    logf = jnp.log(f)
    k = 1.0 - f
    v = hv.reshape(b, t, HG_HEADS, HG_DV)
    o, S = _hgrn2_chunked(q, k, v, logf, s0, prec)
    gn = g_norm.reshape(HG_HEADS, HG_DV)
    o = o * lax.rsqrt(jnp.mean(o * o, axis=-1, keepdims=True) + NORM_EPS) * gn
    o = o.reshape(b, t, HG_VW) * jax.nn.silu(hg)
    return o, S


def _split_proj(p, b, t, pos):
    q = p[:, COL_Q:COL_Q + NSA_W].reshape(b, t, NSA_HEADS, HD)
    q_rot = _rope_partial(q, pos)
    kv6 = p[:, COL_KV:COL_KV + 6 * NSA_KV * HD].reshape(b, t, 6, NSA_KV, HD)
    cmp_rows = kv6[:, :, 0:2]
    slc_rows = jnp.stack([_rope_partial(kv6[:, :, 2], pos), kv6[:, :, 3]], axis=2)
    win_rows = jnp.stack([_rope_partial(kv6[:, :, 4], pos), kv6[:, :, 5]], axis=2)
    ng = p[:, COL_NG:COL_NG + N_GATE].reshape(b, t, N_GATE)
    hq = p[:, COL_HQ:COL_HQ + HG_KW].reshape(b, t, HG_KW)
    hf = p[:, COL_HF:COL_HF + HG_KW].reshape(b, t, HG_KW)
    hv = p[:, COL_HV:COL_HV + HG_VW].reshape(b, t, HG_VW)
    hg = p[:, COL_HGATE:COL_HGATE + HG_VW].reshape(b, t, HG_VW)
    return q, q_rot, cmp_rows, slc_rows, win_rows, ng, (hq, hf, hv, hg)


def _paged_rows(pool, page_table):
    rows = pool[page_table]
    db, n_pages = page_table.shape
    return rows.reshape((db, n_pages * PAGE_SIZE) + rows.shape[3:])


def _reorder_w_in(w):
    o_q, o_kv = 0, NSA_W
    o_ng = o_kv + 6 * NSA_KV * HD
    o_hq = o_ng + N_GATE
    o_ga = o_hq + 2 * HG_KW + 2 * HG_VW
    pad = jnp.zeros((w.shape[0], D_IN_PAD - COL_NG - N_GATE), w.dtype)
    return jnp.concatenate([w[:, o_ga:o_ga + 2 * D_MODEL], w[:, o_q:o_ng], w[:, o_hq:o_ga],
                            w[:, o_ng:o_hq], pad], axis=1)


def _moe_group(top_idx, n_tok):
    n_assign = n_tok * TOP_K
    n_blk = -(-n_assign // MOE_ROWS) + N_EXPERTS
    n_rows = n_blk * MOE_ROWS
    e_flat = top_idx.reshape(-1)
    onehot = (e_flat[:, None] == jnp.arange(N_EXPERTS, dtype=jnp.int32)[None, :]).astype(jnp.int32)
    csum = jnp.cumsum(onehot, axis=0)
    rank = jnp.sum((csum - onehot) * onehot, axis=1)
    counts = csum[-1]
    padded = (counts + MOE_ROWS - 1) // MOE_ROWS * MOE_ROWS
    pend = jnp.cumsum(padded)
    pstart = pend - padded
    dest = pstart[e_flat] + rank
    tok_flat = jnp.arange(n_assign, dtype=jnp.int32) // TOP_K
    row_tok = jnp.full((n_rows,), n_tok, jnp.int32).at[dest].set(tok_flat)
    blk_e = jnp.minimum(jnp.sum(jnp.arange(n_blk)[:, None] * MOE_ROWS >= pend[None, :], axis=1), N_EXPERTS - 1)
    nused = (pend[-1] // MOE_ROWS).reshape(1)
    return row_tok, dest.reshape(n_tok, TOP_K), blk_e.astype(jnp.int32), nused.astype(jnp.int32)


def kernel(x_prompt, x_sample, cache_cmp_kv, cache_slc_kv, cache_win_kv, state_hgrn, cache_mem_kv, page_table, mem_prompt, norm_mix, w_in, cmp_pe_k, cmp_w1_k, cmp_w2_k, cmp_pe_v, cmp_w1_v, cmp_w2_v, lb_logits, hg_norm, w_branch_a, w_branch_b, w_out, norm_x, wq_x, wk_x, wv_x, wo_x, norm_ffn, w_router, b_router, w_gu, b_gu, w_dn, b_dn, norm_final):
    bsz, seq, d = x_prompt.shape
    dbs, dec_seq, _ = x_sample.shape
    past = page_table.shape[1] * PAGE_SIZE
    win_buf_len = cache_win_kv.shape[2]
    n_p = bsz * seq
    n_s = dbs * dec_seq
    pos_p = jnp.arange(seq, dtype=jnp.int32)
    pos_s = past + jnp.arange(dec_seq, dtype=jnp.int32)
    kpos_w = jnp.arange(past - win_buf_len, past + dec_seq, dtype=jnp.int32)
    lb_all = jnp.cumsum(jax.nn.softmax(lb_logits, axis=0), axis=0)
    l = 0
    cmp_par = (cmp_pe_k[l], cmp_w1_k[l], cmp_w2_k[l], cmp_pe_v[l], cmp_w1_v[l], cmp_w2_v[l])
    w_in_r = _reorder_w_in(w_in[l])
    wa, wb, wo = w_branch_a[l], w_branch_b[l], w_out[l]
    xp = x_prompt.reshape(n_p, d)
    xs = x_sample.reshape(n_s, d)

    proj_p = _norm_matmul(xp, norm_mix[l], w_in_r.astype(BF16), normalize=True, hp=False, tm=1024, tn=PROJ_TN)
    q, q_rot, cmp_rows, slc_rows, win_rows, g_nsa, hg_in = _split_proj(proj_p, bsz, seq, pos_p)
    o_win = _nsa_window_banded(q_rot, win_rows, pos_p)
    o_nsa = _nsa_combine(q, q_rot, pos_p, cmp_rows, slc_rows, o_win, g_nsa, cmp_par, None)
    s0 = jnp.zeros((bsz, HG_HEADS, HG_DK, HG_DV), F32)
    o_hg, s_final = _hgrn2_mixer(*hg_in, lb_all[l], hg_norm[l], s0, None)
    xp1 = _merge(xp, o_nsa.reshape(n_p, NSA_W), o_hg.reshape(n_p, HG_VW), proj_p,
                 wa.astype(BF16), wb.astype(BF16), wo.astype(BF16), hp=False, tm=256)

    wkv = jnp.concatenate([wk_x[l], wv_x[l]], axis=1).astype(BF16)
    mem2d = mem_prompt.reshape(bsz * mem_prompt.shape[1], d)
    mkv = _norm_matmul(mem2d, norm_x[l], wkv, normalize=False, hp=False, tm=mem2d.shape[0], tn=PROJ_TN)
    mkv = mkv.reshape(bsz, mem_prompt.shape[1], 2 * X_W)
    xp2 = _cross(xp1.reshape(bsz, seq, d), norm_x[l], mkv, wq_x[l].astype(BF16), wo_x[l].astype(BF16), tm=512)
    xp2 = xp2.reshape(n_p, d)

    proj_s = _norm_matmul(xs, norm_mix[l], w_in_r, normalize=True, hp=True, tm=n_s, tn=PROJ_TN)
    q, q_rot, cmp_new, slc_new, win_new, g_nsa, hg_in = _split_proj(proj_s, dbs, dec_seq, pos_s)
    cmp_all = jnp.concatenate([_paged_rows(cache_cmp_kv[l], page_table), cmp_new], axis=1)
    slc_all = jnp.concatenate([_paged_rows(cache_slc_kv[l], page_table), slc_new], axis=1)
    win_all = jnp.concatenate([cache_win_kv[l], win_new], axis=1)
    o_win = _nsa_window_buffer(q_rot, win_all, pos_s, kpos_w, HI)
    o_nsa_s = _nsa_combine(q, q_rot, pos_s, cmp_all, slc_all, o_win, g_nsa, cmp_par, HI)
    o_hg_s, s_new = _hgrn2_mixer(*hg_in, lb_all[l], hg_norm[l], state_hgrn[l], HI)
    xs1 = _merge(xs, o_nsa_s.reshape(n_s, NSA_W), o_hg_s.reshape(n_s, HG_VW), proj_s, wa, wb, wo, hp=True, tm=n_s)

    hs = _rms(xs1, norm_x[l]).reshape(dbs, dec_seq, d)
    mem_s = cache_mem_kv[l]
    qx = jnp.einsum('btd,dw->btw', hs, wq_x[l], precision=HI).reshape(dbs, dec_seq, X_HEADS, X_HD)
    sx = jnp.einsum('bthd,bmhd->bhtm', qx, mem_s[:, :, 0], precision=HI) * X_SCALE
    px = jax.nn.softmax(sx, axis=-1)
    ox = jnp.einsum('bhtm,bmhd->bthd', px, mem_s[:, :, 1], precision=HI).reshape(n_s, X_W)
    xs2 = xs1 + jnp.dot(ox, wo_x[l], precision=HI)

    n_tok = n_p + n_s
    w_r_pad = jnp.pad(w_router[l], ((0, 0), (0, LANES - N_EXPERTS)))
    b_r_pad = jnp.pad(b_router[l], (0, LANES - N_EXPERTS)).reshape(1, LANES)
    h_p, logit_p = _router(xp2, norm_ffn[l], w_r_pad, b_r_pad, tm=512)
    h_s, logit_s = _router(xs2, norm_ffn[l], w_r_pad, b_r_pad, tm=n_s)
    logits = jnp.concatenate([logit_p, logit_s], axis=0)[:, :N_EXPERTS]
    top_val, top_idx = lax.top_k(logits, TOP_K)
    gates = jax.nn.softmax(top_val, axis=-1)
    row_tok, pos, blk_e, nused = _moe_group(top_idx, n_tok)
    h_all = jnp.concatenate([h_p, h_s, jnp.zeros((1, d), BF16)], axis=0)
    xb = h_all[row_tok]
    yb = _moe_experts(xb, blk_e, nused, w_gu[l], b_gu[l], w_dn[l], b_dn[l])
    yg = yb[pos.T]
    y_prompt = _combine(xp2, yg, gates, norm_final, tm=256, row0=0)
    y_sample = _combine(xs2, yg, gates, norm_final, tm=n_s, row0=n_p)

    mkv_out = mkv.reshape(bsz, mem_prompt.shape[1], 2, X_HEADS, X_HD)
    return (y_prompt.reshape(bsz, seq, d), y_sample.reshape(dbs, dec_seq, d),
            cmp_rows[None], slc_rows[None], win_rows[:, seq - min(WINDOW, seq):][None], s_final[None],
            mkv_out[None], cmp_new[None], slc_new[None],
            win_all[:, win_all.shape[1] - win_buf_len:][None], s_new[None])
```

```python
import functools
import math

import jax
import jax.numpy as jnp
from jax import lax
from jax.experimental import pallas as pl
from jax.experimental.pallas import tpu as pltpu

F32 = jnp.float32
BF16 = jnp.bfloat16
HI = lax.Precision.HIGHEST

D_MODEL = 2048
PAGE_SIZE = 128
NSA_HEADS = 8
NSA_KV = 2
NSA_HG = NSA_HEADS // NSA_KV
HD = 128
CMP_BLOCK = 32
CMP_STRIDE = 16
SLC_BLOCK = 64
SLC_TOP = 16
WINDOW = 512
QBLOCK = 128
ROPE_DIM = HD // 4
ROPE_THETA = 500000.0
SCALE = HD ** -0.5
HG_HEADS = 8
HG_DK = 128
HG_DV = 128
HG_CHUNK = 64
X_HEADS = 4
X_HD = 128
X_W = X_HEADS * X_HD
X_SCALE = X_HD ** -0.5
N_EXPERTS = 32
TOP_K = 4
D_FF = D_MODEL
SWIGLU_LIMIT = 7.0
SWIGLU_ALPHA = 1.702
NORM_EPS = 1e-5
NSA_W = NSA_HEADS * HD
HG_KW = HG_HEADS * HG_DK
HG_VW = HG_HEADS * HG_DV

LANES = 128
MXU_N = 256
VMEM_LIMIT = 52 * 1024 * 1024

COL_GA = 0
COL_GB = COL_GA + D_MODEL
COL_Q = COL_GB + D_MODEL
COL_KV = COL_Q + NSA_W
COL_HQ = COL_KV + 6 * NSA_KV * HD
COL_HF = COL_HQ + HG_KW
COL_HV = COL_HF + HG_KW
COL_HGATE = COL_HV + HG_VW
COL_NG = COL_HGATE + HG_VW
N_GATE = NSA_HEADS * 3
PROJ_TN = 512
D_IN_PAD = -(-(COL_NG + N_GATE) // PROJ_TN) * PROJ_TN

MOE_ROWS = 256
MOE_TN = 1024


def _params(sem):
    return pltpu.CompilerParams(dimension_semantics=sem, vmem_limit_bytes=VMEM_LIMIT)


def _rms(x, g):
    return x * lax.rsqrt(jnp.mean(x * x, axis=-1, keepdims=True) + NORM_EPS) * g


def _dot(a, b, hp):
    if hp:
        return jnp.dot(a.astype(F32), b.astype(F32), precision=HI, preferred_element_type=F32)
    return jnp.dot(a.astype(BF16), b.astype(BF16), preferred_element_type=F32)


def _norm_matmul_kernel(x_ref, g_ref, w_ref, o_ref, h_ref, *, normalize, hp):
    @pl.when(pl.program_id(1) == 0)
    def _():
        x = x_ref[...]
        if normalize:
            x = _rms(x, g_ref[...])
        h_ref[...] = x.astype(h_ref.dtype)

    o_ref[...] = _dot(h_ref[...], w_ref[...], hp)


def _norm_matmul(x, g, w, *, normalize, hp, tm, tn):
    m, k = x.shape
    n = w.shape[1]
    return pl.pallas_call(
        functools.partial(_norm_matmul_kernel, normalize=normalize, hp=hp),
        out_shape=jax.ShapeDtypeStruct((m, n), F32),
        grid=(m // tm, n // tn),
        in_specs=[pl.BlockSpec((tm, k), lambda i, j: (i, 0)),
                  pl.BlockSpec((1, k), lambda i, j: (0, 0)),
                  pl.BlockSpec((k, tn), lambda i, j: (0, j))],
        out_specs=pl.BlockSpec((tm, tn), lambda i, j: (i, j)),
        scratch_shapes=[pltpu.VMEM((tm, k), F32 if hp else BF16)],
        compiler_params=_params(("parallel", "arbitrary")),
    )(x, g.reshape(1, k), w)


def _merge_kernel(x_ref, on_ref, oh_ref, ga_ref, gb_ref, wa_ref, wb_ref, wo_ref, o_ref, *, hp):
    a = _dot(on_ref[...], wa_ref[...], hp)
    b = _dot(oh_ref[...], wb_ref[...], hp)
    m = jax.nn.sigmoid(ga_ref[...]) * a + jax.nn.sigmoid(gb_ref[...]) * b
    o_ref[...] = x_ref[...] + _dot(m, wo_ref[...], hp)


def _merge(x, o_nsa, o_hg, proj, wa, wb, wo, *, hp, tm):
    m = x.shape[0]
    d = D_MODEL
    const = lambda i: (0, 0)
    return pl.pallas_call(
        functools.partial(_merge_kernel, hp=hp),
        out_shape=jax.ShapeDtypeStruct((m, d), F32),
        grid=(m // tm,),
        in_specs=[pl.BlockSpec((tm, d), lambda i: (i, 0)),
                  pl.BlockSpec((tm, NSA_W), lambda i: (i, 0)),
                  pl.BlockSpec((tm, HG_VW), lambda i: (i, 0)),
                  pl.BlockSpec((tm, d), lambda i: (i, COL_GA // d)),
                  pl.BlockSpec((tm, d), lambda i: (i, COL_GB // d)),
                  pl.BlockSpec((NSA_W, d), const, pipeline_mode=pl.Buffered(1)),
                  pl.BlockSpec((HG_VW, d), const, pipeline_mode=pl.Buffered(1)),
                  pl.BlockSpec((d, d), const, pipeline_mode=pl.Buffered(1))],
        out_specs=pl.BlockSpec((tm, d), lambda i: (i, 0)),
        compiler_params=_params(("parallel",)),
    )(x, o_nsa, o_hg, proj, proj, wa, wb, wo)


def _cross_kernel(x_ref, g_ref, k_ref, v_ref, wq_ref, wo_ref, o_ref):
    x = x_ref[0]
    h = _rms(x, g_ref[...])
    q = _dot(h, wq_ref[...], False)
    k = k_ref[0].astype(BF16)
    v = v_ref[0].astype(BF16)
    outs = []
    for hd in range(X_HEADS):
        sl = slice(hd * X_HD, (hd + 1) * X_HD)
        s = lax.dot_general(q[:, sl].astype(BF16), k[:, sl], (((1,), (1,)), ((), ())),
                            preferred_element_type=F32) * X_SCALE
        s = s - jnp.max(s, axis=-1, keepdims=True)
        e = jnp.exp(s)
        p = e / jnp.sum(e, axis=-1, keepdims=True)
        outs.append(jnp.dot(p.astype(BF16), v[:, sl], preferred_element_type=F32))
    o = jnp.concatenate(outs, axis=-1)
    o_ref[0] = x + _dot(o, wo_ref[...], False)


def _cross(x, g, mkv, wq, wo, *, tm):
    b, t, d = x.shape
    mlen = mkv.shape[1]
    const = lambda bi, i: (0, 0)
    return pl.pallas_call(
        _cross_kernel,
        out_shape=jax.ShapeDtypeStruct((b, t, d), F32),
        grid=(b, t // tm),
        in_specs=[pl.BlockSpec((1, tm, d), lambda bi, i: (bi, i, 0)),
                  pl.BlockSpec((1, d), const),
                  pl.BlockSpec((1, mlen, X_W), lambda bi, i: (bi, 0, 0)),
                  pl.BlockSpec((1, mlen, X_W), lambda bi, i: (bi, 0, 1)),
                  pl.BlockSpec((d, X_W), const),
                  pl.BlockSpec((X_W, d), const)],
        out_specs=pl.BlockSpec((1, tm, d), lambda bi, i: (bi, i, 0)),
        compiler_params=_params(("parallel", "parallel")),
    )(x, g.reshape(1, d), mkv, mkv, wq, wo)


def _router_kernel(x_ref, g_ref, w_ref, b_ref, h_ref, l_ref):
    h = _rms(x_ref[...], g_ref[...])
    h_ref[...] = h.astype(BF16)
    l_ref[...] = _dot(h, w_ref[...], True) + b_ref[...]


def _router(x, g, w_r_pad, b_r_pad, *, tm):
    m, d = x.shape
    const = lambda i: (0, 0)
    return pl.pallas_call(
        _router_kernel,
        out_shape=(jax.ShapeDtypeStruct((m, d), BF16), jax.ShapeDtypeStruct((m, LANES), F32)),
        grid=(m // tm,),
        in_specs=[pl.BlockSpec((tm, d), lambda i: (i, 0)),
                  pl.BlockSpec((1, d), const),
                  pl.BlockSpec((d, LANES), const),
                  pl.BlockSpec((1, LANES), const)],
        out_specs=(pl.BlockSpec((tm, d), lambda i: (i, 0)), pl.BlockSpec((tm, LANES), lambda i: (i, 0))),
        compiler_params=_params(("parallel",)),
    )(x, g.reshape(1, d), w_r_pad, b_r_pad)


def _new_weight_tile(blk_e_ref, i):
    prev = blk_e_ref[jnp.maximum(i - 1, 0)]
    return (i == 0) | (blk_e_ref[i] != prev)


def _moe_gu_kernel(blk_e_ref, nused_ref, x_ref, w_ref, b_ref, sel_ref, o_ref, wbf_ref):
    i = pl.program_id(1)
    tn = w_ref.shape[2]

    @pl.when(_new_weight_tile(blk_e_ref, i))
    def _():
        wbf_ref[...] = w_ref[0].astype(BF16)

    @pl.when(i < nused_ref[0])
    def _():
        gu = jnp.dot(x_ref[...], wbf_ref[...], preferred_element_type=F32) + b_ref[0]
        gate = jnp.minimum(gu, SWIGLU_LIMIT)
        up = jnp.clip(gu, -SWIGLU_LIMIT, SWIGLU_LIMIT)
        up_next = pltpu.roll(up, tn - 1, axis=1)
        act = (up_next + 1.0) * gate * jax.nn.sigmoid(gate * SWIGLU_ALPHA)
        lane = lax.broadcasted_iota(jnp.int32, act.shape, 1)
        act = jnp.where(lane % 2 == 0, act, 0.0).astype(BF16)
        for c in range(tn // MXU_N):
            o_ref[:, c * LANES:(c + 1) * LANES] = jnp.dot(
                act[:, c * MXU_N:(c + 1) * MXU_N], sel_ref[...], preferred_element_type=F32).astype(BF16)

    @pl.when(i >= nused_ref[0])
    def _():
        o_ref[...] = jnp.zeros_like(o_ref)


def _moe_dn_kernel(blk_e_ref, nused_ref, a_ref, w_ref, b_ref, o_ref, wbf_ref):
    i = pl.program_id(1)

    @pl.when(_new_weight_tile(blk_e_ref, i))
    def _():
        wbf_ref[...] = w_ref[0].astype(BF16)

    @pl.when(i < nused_ref[0])
    def _():
        o_ref[...] = jnp.dot(a_ref[...], wbf_ref[...], preferred_element_type=F32) + b_ref[0]

    @pl.when(i >= nused_ref[0])
    def _():
        o_ref[...] = jnp.zeros_like(o_ref)


def _moe_experts(xb, blk_e, nused, w_gu, b_gu, w_dn, b_dn):
    n_rows, d = xb.shape
    n_blk = n_rows // MOE_ROWS
    tn = MOE_TN
    sel = (jnp.arange(MXU_N)[:, None] == 2 * jnp.arange(LANES)[None, :]).astype(BF16)
    act = pl.pallas_call(
        _moe_gu_kernel,
        out_shape=jax.ShapeDtypeStruct((n_rows, D_FF), BF16),
        grid_spec=pltpu.PrefetchScalarGridSpec(
            num_scalar_prefetch=2,
            grid=(2 * D_FF // tn, n_blk),
            in_specs=[pl.BlockSpec((MOE_ROWS, d), lambda j, i, be, nu: (i, 0)),
                      pl.BlockSpec((1, d, tn), lambda j, i, be, nu: (be[i], 0, j)),
                      pl.BlockSpec((1, 1, tn), lambda j, i, be, nu: (be[i], 0, j)),
                      pl.BlockSpec((MXU_N, LANES), lambda j, i, be, nu: (0, 0))],
            out_specs=pl.BlockSpec((MOE_ROWS, tn // 2), lambda j, i, be, nu: (i, j)),
            scratch_shapes=[pltpu.VMEM((d, tn), BF16)]),
        compiler_params=_params(("arbitrary", "arbitrary")),
    )(blk_e, nused, xb, w_gu, b_gu.reshape(N_EXPERTS, 1, 2 * D_FF), sel)
    return pl.pallas_call(
        _moe_dn_kernel,
        out_shape=jax.ShapeDtypeStruct((n_rows, d), F32),
        grid_spec=pltpu.PrefetchScalarGridSpec(
            num_scalar_prefetch=2,
            grid=(d // tn, n_blk),
            in_specs=[pl.BlockSpec((MOE_ROWS, D_FF), lambda j, i, be, nu: (i, 0)),
                      pl.BlockSpec((1, D_FF, tn), lambda j, i, be, nu: (be[i], 0, j)),
                      pl.BlockSpec((1, 1, tn), lambda j, i, be, nu: (be[i], 0, j))],
            out_specs=pl.BlockSpec((MOE_ROWS, tn), lambda j, i, be, nu: (i, j)),
            scratch_shapes=[pltpu.VMEM((D_FF, tn), BF16)]),
        compiler_params=_params(("arbitrary", "arbitrary")),
    )(blk_e, nused, act, w_dn, b_dn.reshape(N_EXPERTS, 1, d))


def _combine_kernel(x_ref, y_ref, gt_ref, g_ref, o_ref):
    acc = x_ref[...]
    gt = gt_ref[...]
    for k in range(TOP_K):
        acc = acc + y_ref[k] * gt[:, k:k + 1]
    o_ref[...] = _rms(acc, g_ref[...])


def _combine(x, yg, gates, g, *, tm, row0):
    m, d = x.shape
    off = row0 // tm
    return pl.pallas_call(
        _combine_kernel,
        out_shape=jax.ShapeDtypeStruct((m, d), F32),
        grid=(m // tm,),
        in_specs=[pl.BlockSpec((tm, d), lambda i: (i, 0)),
                  pl.BlockSpec((TOP_K, tm, d), lambda i: (0, i + off, 0)),
                  pl.BlockSpec((tm, TOP_K), lambda i: (i + off, 0)),
                  pl.BlockSpec((1, d), lambda i: (0, 0))],
        out_specs=pl.BlockSpec((tm, d), lambda i: (i, 0)),
        compiler_params=_params(("parallel",)),
    )(x, yg, gates, g.reshape(1, d))


def _rope_tile(x, cos_t, sin_lo, sin_hi):
    half = ROPE_DIM // 2
    return x * cos_t + pltpu.roll(x, half, axis=1) * sin_hi + pltpu.roll(x, HD - half, axis=1) * sin_lo


def _nsa_prep_kernel(q_ref, cmp_ref, slc_ref, win_ref, cos_ref, slo_ref, shi_ref,
                     cmp_o, slc_o, win_o, qtc_o, qtr_o, ks_o, vst_o, kw_o, vwt_o):
    cos_t, sin_lo, sin_hi = cos_ref[...], slo_ref[...], shi_ref[...]
    cmp_o[...] = cmp_ref[...]
    for g in range(NSA_KV):
        for h in range(NSA_HG):
            col = (g * NSA_HG + h) * HD
            qh = q_ref[:, col:col + HD]
            qtc_o[0, g, 0, :, h * HD:(h + 1) * HD] = qh.T.astype(BF16)
            qtr_o[0, g, 0, :, h * HD:(h + 1) * HD] = _rope_tile(qh, cos_t, sin_lo, sin_hi).T.astype(BF16)
        kcol = slice(g * HD, (g + 1) * HD)
        vcol = slice((NSA_KV + g) * HD, (NSA_KV + g + 1) * HD)
        for src, rows_o, k_o, vt_o in ((slc_ref, slc_o, ks_o, vst_o), (win_ref, win_o, kw_o, vwt_o)):
            k_rot = _rope_tile(src[:, kcol], cos_t, sin_lo, sin_hi)
            v = src[:, vcol]
            rows_o[:, kcol] = k_rot
            rows_o[:, vcol] = v
            k_o[0, g, 0] = k_rot.astype(BF16)
            vt_o[0, g, 0] = v.T.astype(BF16)


def _rope_tables(pos):
    half = ROPE_DIM // 2
    inv = ROPE_THETA ** (-jnp.arange(0, ROPE_DIM, 2, dtype=F32) / ROPE_DIM)
    ang = pos.astype(F32)[:, None] * inv[None, :]
    cos, sin = jnp.cos(ang), jnp.sin(ang)
    t = pos.shape[0]
    ones = jnp.ones((t, HD - ROPE_DIM), F32)
    zeros = jnp.zeros((t, HD - ROPE_DIM), F32)
    zh = jnp.zeros((t, half), F32)
    cos_t = jnp.concatenate([cos, cos, ones], axis=1)
    sin_lo = jnp.concatenate([-sin, zh, zeros], axis=1)
    sin_hi = jnp.concatenate([zh, sin, zeros], axis=1)
    return cos_t, sin_lo, sin_hi


def _nsa_prep(proj, bsz, seq, pos):
    n = bsz * seq
    tq = QBLOCK
    nq = seq // tq
    kvw = 2 * NSA_KV * HD
    cos_t, sin_lo, sin_hi = _rope_tables(pos)
    row = lambda r: (r, 0)
    tab = lambda r: (r % nq, 0)
    blk5 = lambda r: (r // nq, 0, r % nq, 0, 0)
    qt_shape = jax.ShapeDtypeStruct((bsz, NSA_KV, nq, HD, NSA_HG * tq), BF16)
    kv_shape = jax.ShapeDtypeStruct((bsz, NSA_KV, nq, tq, HD), BF16)
    rows_shape = jax.ShapeDtypeStruct((n, kvw), F32)
    return pl.pallas_call(
        _nsa_prep_kernel,
        out_shape=(rows_shape, rows_shape, rows_shape, qt_shape, qt_shape, kv_shape, kv_shape, kv_shape, kv_shape),
        grid=(n // tq,),
        in_specs=[pl.BlockSpec((tq, NSA_W), lambda r: (r, COL_Q // NSA_W)),
                  pl.BlockSpec((tq, kvw), lambda r: (r, COL_KV // kvw)),
                  pl.BlockSpec((tq, kvw), lambda r: (r, COL_KV // kvw + 1)),
                  pl.BlockSpec((tq, kvw), lambda r: (r, COL_KV // kvw + 2)),
                  pl.BlockSpec((tq, HD), tab), pl.BlockSpec((tq, HD), tab), pl.BlockSpec((tq, HD), tab)],
        out_specs=(pl.BlockSpec((tq, kvw), row), pl.BlockSpec((tq, kvw), row), pl.BlockSpec((tq, kvw), row),
                   pl.BlockSpec((1, NSA_KV, 1, HD, NSA_HG * tq), blk5),
                   pl.BlockSpec((1, NSA_KV, 1, HD, NSA_HG * tq), blk5),
                   pl.BlockSpec((1, NSA_KV, 1, tq, HD), blk5), pl.BlockSpec((1, NSA_KV, 1, HD, tq), blk5),
                   pl.BlockSpec((1, NSA_KV, 1, tq, HD), blk5), pl.BlockSpec((1, NSA_KV, 1, HD, tq), blk5)),
        compiler_params=_params(("parallel",)),
    )(proj, proj, proj, proj, cos_t, sin_lo, sin_hi)


def _compress_kernel(x_ref, pe_ref, w1_ref, w2_ref, nat_o, tr_o):
    nch = x_ref.shape[1] // CMP_STRIDE
    acc = jnp.zeros((nch, 2 * HD), F32)
    bias = [jnp.zeros((8, HD), F32) for _ in range(CMP_BLOCK // CMP_STRIDE)]
    for s in range(CMP_STRIDE):
        w = w1_ref[0, s]
        rows = x_ref[0, pl.ds(s, nch, stride=CMP_STRIDE), :]
        acc = acc + jnp.dot(rows.astype(BF16), w, preferred_element_type=F32)
        for r in range(CMP_BLOCK // CMP_STRIDE):
            pe_row = jnp.broadcast_to(pe_ref[0, r * CMP_STRIDE + s:r * CMP_STRIDE + s + 1, :], (8, HD))
            pb = jnp.dot(pe_row.astype(BF16), w, preferred_element_type=F32)
            bias[r] = bias[r] + pb[:, r * HD:(r + 1) * HD]
    first = acc[:, :HD] + bias[0][0:1]
    second = acc[:, HD:] + bias[1][0:1]
    pre = first + pltpu.roll(second, nch - 1, axis=0)
    out = jnp.dot(jax.nn.silu(pre).astype(BF16), w2_ref[0], preferred_element_type=F32)
    row = lax.broadcasted_iota(jnp.int32, out.shape, 0)
    out = jnp.where(row < nch - 1, out, 0.0)
    nat_o[0, 0] = out.astype(BF16)
    tr_o[0, 0] = out.T.astype(BF16)


def _compress_weights(cmp_par):
    pe_k, w1_k, w2_k, pe_v, w1_v, w2_v = cmp_par

    def cat(w1):
        return jnp.concatenate([w1[:CMP_STRIDE], w1[CMP_STRIDE:]], axis=2)

    pe = jnp.stack([pe_k, pe_v])
    w1 = jnp.stack([cat(w1_k), cat(w1_v)]).astype(BF16)
    w2 = jnp.stack([w2_k, w2_v]).astype(BF16)
    return pe, w1, w2


def _nsa_compress_rows(rows, cmp_par):
    bsz, length, _ = rows.shape
    nch = length // CMP_STRIDE
    ncol = 2 * NSA_KV
    pe, w1, w2 = _compress_weights(cmp_par)
    return pl.pallas_call(
        _compress_kernel,
        out_shape=(jax.ShapeDtypeStruct((bsz, ncol, nch, HD), BF16), jax.ShapeDtypeStruct((bsz, ncol, HD, nch), BF16)),
        grid=(bsz, ncol),
        in_specs=[pl.BlockSpec((1, length, HD), lambda b, c: (b, 0, c)),
                  pl.BlockSpec((1, CMP_BLOCK, HD), lambda b, c: (c // NSA_KV, 0, 0)),
                  pl.BlockSpec((1, CMP_STRIDE, HD, 2 * HD), lambda b, c: (c // NSA_KV, 0, 0, 0)),
                  pl.BlockSpec((1, HD, HD), lambda b, c: (c // NSA_KV, 0, 0))],
        out_specs=(pl.BlockSpec((1, 1, nch, HD), lambda b, c: (b, c, 0, 0)),
                   pl.BlockSpec((1, 1, HD, nch), lambda b, c: (b, c, 0, 0))),
        compiler_params=_params(("parallel", "parallel")),
    )(rows, pe, w1, w2)


def _imp_matrix_t(nch, ns):
    a = SLC_BLOCK // CMP_STRIDE
    b = CMP_BLOCK // CMP_STRIDE
    j = jnp.arange(ns)[:, None]
    n = jnp.arange(nch)[None, :]
    out = jnp.zeros((ns, nch), F32)
    for m in range(a):
        for k in range(b):
            out = out + (n == a * j + a - 1 - m - k).astype(F32)
    return out


def _flash_t(k_ref, vt_ref, qt, acc_ref, lo, hi, mask_fn):
    width = qt.shape[1]
    acc_ref[...] = jnp.zeros_like(acc_ref)

    def body(kt, carry):
        m_old, l_old = carry
        s = jnp.dot(k_ref[0, 0, kt], qt, preferred_element_type=F32) * SCALE
        s = jnp.where(mask_fn(kt), s, -jnp.inf)
        m_new = jnp.maximum(m_old, jnp.max(s, axis=0, keepdims=True))
        m_safe = jnp.where(m_new == -jnp.inf, 0.0, m_new)
        alpha = jnp.exp(m_old - m_safe)
        p = jnp.exp(s - m_safe)
        l_new = alpha * l_old + jnp.sum(p, axis=0, keepdims=True)
        acc_ref[...] = alpha * acc_ref[...] + jnp.dot(vt_ref[0, 0, kt], p.astype(BF16), preferred_element_type=F32)
        return m_new, l_new

    init = (jnp.full((1, width), -jnp.inf, F32), jnp.zeros((1, width), F32))
    _, l_fin = lax.fori_loop(lo, hi, body, init)
    return acc_ref[...] / jnp.where(l_fin > 0, l_fin, 1.0)


def _nsa_attn_kernel(qtc_ref, qtr_ref, kc_ref, vct_ref, imp_ref, ks_ref, vst_ref, kw_ref, vwt_ref, ng_ref,
                     o_ref, acc_ref, sel_ref):
    qi = pl.program_id(2)
    tq = QBLOCK
    nch = kc_ref.shape[2]
    ns = imp_ref.shape[0]
    tile4 = lambda a: jnp.concatenate([a] * NSA_HG, axis=1)
    pos = qi * tq + lax.broadcasted_iota(jnp.int32, (1, tq), 1)

    s = jnp.dot(kc_ref[0, 0], qtc_ref[0, 0, 0], preferred_element_type=F32) * SCALE
    cmp_end = lax.broadcasted_iota(jnp.int32, (nch, tq), 0) * CMP_STRIDE + (CMP_BLOCK - 1)
    mask_c = tile4(cmp_end <= pos)
    s = jnp.where(mask_c, s, -jnp.inf)
    m = jnp.max(s, axis=0, keepdims=True)
    m = jnp.where(m == -jnp.inf, 0.0, m)
    e = jnp.exp(s - m)
    d = jnp.sum(e, axis=0, keepdims=True)
    p = e / jnp.where(d > 0, d, 1.0)
    o_c = jnp.dot(vct_ref[0, 0], p.astype(BF16), preferred_element_type=F32)
    p_grp = p[:, 0:tq]
    for h in range(1, NSA_HG):
        p_grp = p_grp + p[:, h * tq:(h + 1) * tq]
    imp = jnp.dot(imp_ref[...], p_grp, precision=HI, preferred_element_type=F32)

    blk = lax.broadcasted_iota(jnp.int32, (ns, tq), 0)
    cur = pos // SLC_BLOCK
    forced = (blk == 0) | (blk == cur) | (blk == cur - 1)
    score = jnp.where(forced, jnp.inf, imp)
    score = jnp.where(blk <= cur, score, -jnp.inf)
    sel = jnp.zeros((ns, tq), F32)
    for _ in range(min(SLC_TOP, ns)):
        best = jnp.max(score, axis=0, keepdims=True)
        idx = jnp.min(jnp.where(score == best, blk, ns), axis=0, keepdims=True)
        pick = blk == idx
        sel = jnp.where(pick, 1.0, sel)
        score = jnp.where(pick, -jnp.inf, score)
    sel_ref[...] = sel

    key_in_tile = lax.broadcasted_iota(jnp.int32, (tq, tq), 0)
    per_tile = tq // SLC_BLOCK

    def mask_slc(kt):
        chosen = sel_ref[pl.ds(kt * per_tile, 1), :]
        for b in range(1, per_tile):
            chosen = jnp.where(key_in_tile >= b * SLC_BLOCK, sel_ref[pl.ds(kt * per_tile + b, 1), :], chosen)
        return tile4((chosen > 0.5) & (kt * tq + key_in_tile <= pos))

    def mask_win(kt):
        delta = pos - (kt * tq + key_in_tile)
        return tile4((delta >= 0) & (delta <= WINDOW))

    qtr = qtr_ref[0, 0, 0]
    o_s = _flash_t(ks_ref, vst_ref, qtr, acc_ref, 0, qi + 1, mask_slc)
    o_w = _flash_t(kw_ref, vwt_ref, qtr, acc_ref, jnp.maximum(qi - WINDOW // tq, 0), qi + 1, mask_win)

    gates = jax.nn.sigmoid(ng_ref[0, 0])
    for h in range(NSA_HG):
        gate = gates[3 * h:3 * h + 3]
        sl = slice(h * tq, (h + 1) * tq)
        o = gate[0:1] * o_c[:, sl] + gate[1:2] * o_s[:, sl] + gate[2:3] * o_w[:, sl]
        o_ref[0, :, h * HD:(h + 1) * HD] = o.T.astype(o_ref.dtype)


def _nsa_attn(qtc, qtr, kc_nat, kc_tr, ks5, vst5, kw5, vwt5, ng_t):
    bsz, _, nq, _, width = qtc.shape
    tq = QBLOCK
    seq = nq * tq
    nch = kc_nat.shape[2]
    ns = seq // SLC_BLOCK
    imp_t = _imp_matrix_t(nch, ns)
    q_spec = pl.BlockSpec((1, 1, 1, HD, width), lambda b, g, i: (b, g, i, 0, 0))
    k5_spec = pl.BlockSpec((1, 1, nq, tq, HD), lambda b, g, i: (b, g, 0, 0, 0))
    return pl.pallas_call(
        _nsa_attn_kernel,
        out_shape=jax.ShapeDtypeStruct((bsz, seq, NSA_W), BF16),
        grid=(bsz, NSA_KV, nq),
        in_specs=[q_spec, q_spec,
                  pl.BlockSpec((1, 1, nch, HD), lambda b, g, i: (b, g, 0, 0)),
                  pl.BlockSpec((1, 1, HD, nch), lambda b, g, i: (b, NSA_KV + g, 0, 0)),
                  pl.BlockSpec((ns, nch), lambda b, g, i: (0, 0)),
                  k5_spec, k5_spec, k5_spec, k5_spec,
                  pl.BlockSpec((1, 1, N_GATE // NSA_KV, tq), lambda b, g, i: (b, g, 0, i))],
        out_specs=pl.BlockSpec((1, tq, NSA_HG * HD), lambda b, g, i: (b, i, g)),
        scratch_shapes=[pltpu.VMEM((HD, width), F32), pltpu.VMEM((ns, tq), F32)],
        compiler_params=_params(("parallel", "parallel", "arbitrary")),
    )(qtc, qtr, kc_nat, kc_tr, imp_t, ks5, vst5, kw5, vwt5, ng_t)


def _nsa_prompt(proj, bsz, seq, pos, cmp_par):
    cmp_rows, slc_rows, win_rows, qtc, qtr, ks5, vst5, kw5, vwt5 = _nsa_prep(proj, bsz, seq, pos)
    kc_nat, kc_tr = _nsa_compress_rows(cmp_rows.reshape(bsz, seq, -1), cmp_par)
    ng_t = proj[:, COL_NG:COL_NG + N_GATE].reshape(bsz, seq, NSA_KV, N_GATE // NSA_KV).transpose(0, 2, 3, 1)
    o_nsa = _nsa_attn(qtc, qtr, kc_nat, kc_tr, ks5, vst5, kw5, vwt5, ng_t)
    shape5 = (bsz, seq, 2, NSA_KV, HD)
    return o_nsa, cmp_rows.reshape(shape5), slc_rows.reshape(shape5), win_rows.reshape(shape5)


HG_TILE = 128
HG_DIAG = 8


def _split3(x):
    hi = x.astype(BF16)
    r1 = x - hi.astype(F32)
    mid = r1.astype(BF16)
    lo = (r1 - mid.astype(F32)).astype(BF16)
    return hi, mid, lo


def _hgrn2_kernel(hq_ref, hf_ref, hv_ref, hg_ref, lb_ref, gn_ref, o_ref, s_ref, st_ref):
    i = pl.program_id(2)
    c = HG_TILE

    @pl.when(i == 0)
    def _():
        st_ref[...] = jnp.zeros_like(st_ref)

    lb = lb_ref[0]
    q = hq_ref[...]
    f = lb + (1.0 - lb) * jax.nn.sigmoid(hf_ref[...])
    logf = jnp.log(f)
    k = 1.0 - f
    v = hv_ref[...]
    row = lax.broadcasted_iota(jnp.int32, (c, c), 0)
    col = lax.broadcasted_iota(jnp.int32, (c, c), 1)
    tri = (row >= col).astype(BF16)
    b = sum(jnp.dot(tri, part, preferred_element_type=F32) for part in _split3(logf))
    b_last = b[c - 1:c, :]

    o = (jnp.sum(q * k, axis=1, keepdims=True)) * v
    for delta in range(1, HG_DIAG):
        valid = (row % HG_DIAG) >= delta
        decay = jnp.exp(jnp.where(valid, b - pltpu.roll(b, delta, axis=0), -jnp.inf))
        a = jnp.sum(q * pltpu.roll(k, delta, axis=0) * decay, axis=1, keepdims=True)
        o = o + a * pltpu.roll(v, delta, axis=0)

    a_mat = jnp.zeros((c, c), F32)
    blk = 2 * HG_DIAG
    while blk <= c:
        half = blk // 2
        b_mid = jnp.concatenate(
            [jnp.broadcast_to(b[s + half - 1:s + half, :], (blk, b.shape[1])) for s in range(0, c, blk)], axis=0)
        second = (row % blk) >= half
        qs = q * jnp.exp(jnp.where(second, b - b_mid, -jnp.inf))
        ks = k * jnp.exp(jnp.where(second, -jnp.inf, b_mid - b))
        a_blk = lax.dot_general(qs.astype(BF16), ks.astype(BF16), (((1,), (1,)), ((), ())),
                                preferred_element_type=F32)
        a_mat = a_mat + jnp.where((row // blk) == (col // blk), a_blk, 0.0)
        blk *= 2

    st = st_ref[...]
    o = o + jnp.dot(a_mat.astype(BF16), v.astype(BF16), preferred_element_type=F32)
    o = o + lax.dot_general((q * jnp.exp(b)).astype(BF16), st.astype(BF16), (((1,), (1,)), ((), ())),
                            preferred_element_type=F32)
    k_end = (k * jnp.exp(b_last - b)).astype(BF16)
    st_new = st * jnp.exp(b_last) + jnp.dot(v.T.astype(BF16), k_end, preferred_element_type=F32)
    st_ref[...] = st_new

    o = o * lax.rsqrt(jnp.mean(o * o, axis=1, keepdims=True) + NORM_EPS) * gn_ref[0]
    o_ref[...] = (o * jax.nn.silu(hg_ref[...])).astype(o_ref.dtype)

    @pl.when(i == pl.num_programs(2) - 1)
    def _():
        s_ref[0, 0] = st_new.T


def _hgrn2_prompt(proj, bsz, seq, lb, g_norm):
    c = HG_TILE
    nc = seq // c
    col = lambda base: (lambda b, h, i: (b * nc + i, base // HG_DK + h))
    vec = pl.BlockSpec((1, 1, HG_DK), lambda b, h, i: (h, 0, 0))
    return pl.pallas_call(
        _hgrn2_kernel,
        out_shape=(jax.ShapeDtypeStruct((bsz * seq, HG_VW), BF16),
                   jax.ShapeDtypeStruct((bsz, HG_HEADS, HG_DK, HG_DV), F32)),
        grid=(bsz, HG_HEADS, nc),
        in_specs=[pl.BlockSpec((c, HG_DK), col(COL_HQ)), pl.BlockSpec((c, HG_DK), col(COL_HF)),
                  pl.BlockSpec((c, HG_DV), col(COL_HV)), pl.BlockSpec((c, HG_DV), col(COL_HGATE)), vec, vec],
        out_specs=(pl.BlockSpec((c, HG_DV), lambda b, h, i: (b * nc + i, h)),
                   pl.BlockSpec((1, 1, HG_DK, HG_DV), lambda b, h, i: (b, h, 0, 0))),
        scratch_shapes=[pltpu.VMEM((HG_DV, HG_DK), F32)],
        compiler_params=_params(("parallel", "parallel", "arbitrary")),
    )(proj, proj, proj, proj, lb.reshape(HG_HEADS, 1, HG_DK), g_norm.reshape(HG_HEADS, 1, HG_DV))


def _rope_partial(x, pos):
    half = ROPE_DIM // 2
    inv = ROPE_THETA ** (-jnp.arange(0, ROPE_DIM, 2, dtype=F32) / ROPE_DIM)
    ang = pos.astype(F32)[:, None] * inv[None, :]
    cos = jnp.cos(ang)[:, None, :]
    sin = jnp.sin(ang)[:, None, :]
    xr = x[..., :ROPE_DIM]
    x1, x2 = xr[..., :half], xr[..., half:]
    rot = jnp.concatenate([x1 * cos - x2 * sin, x2 * cos + x1 * sin], axis=-1)
    return jnp.concatenate([rot, x[..., ROPE_DIM:]], axis=-1)


def _masked_softmax(s, mask):
    s = jnp.where(mask, s, -jnp.inf)
    m = jnp.max(s, axis=-1, keepdims=True)
    m = jnp.where(jnp.isfinite(m), m, 0.0)
    e = jnp.where(mask, jnp.exp(s - m), 0.0)
    d = jnp.sum(e, axis=-1, keepdims=True)
    return e / jnp.where(d > 0, d, 1.0)


def _nsa_compress(rows, pe, w1, w2, prec):
    b, l = rows.shape[:2]
    n_sub = CMP_BLOCK // CMP_STRIDE
    nc = (l - CMP_BLOCK) // CMP_STRIDE + 1
    nch = l // CMP_STRIDE
    ch = rows[:, :nch * CMP_STRIDE].reshape(b, nch, CMP_STRIDE, NSA_KV, HD)
    acc = 0.0
    for r in range(n_sub):
        sl = slice(r * CMP_STRIDE, (r + 1) * CMP_STRIDE)
        pr = jnp.einsum('bnsgd,sdh->bngh', ch + pe[sl][:, None, :], w1[sl], precision=prec)
        acc = acc + pr[:, r:r + nc]
    return jnp.einsum('bngh,hd->bngd', jax.nn.silu(acc), w2, precision=prec)


def _cmp_to_slc(p_cmp, ns):
    a = SLC_BLOCK // CMP_STRIDE
    b = CMP_BLOCK // CMP_STRIDE
    nc = p_cmp.shape[-1]
    total = a * ns + a + b
    pad_cfg = [(0, 0)] * (p_cmp.ndim - 1) + [(b - 1, total - (b - 1) - nc)]
    pp = jnp.pad(p_cmp, pad_cfg)
    out = 0.0
    for m in range(a):
        for n in range(b):
            st = a - 1 - m - n + (b - 1)
            out = out + pp[..., st:st + a * ns:a]
    return out


def _nsa_cmp_slc(q, q_rot, pos, kc, vc, ks_rows, vs_rows, prec):
    b, t = q.shape[:2]
    l = ks_rows.shape[1]
    nc = kc.shape[1]
    ns = -(-l // SLC_BLOCK)
    nsel = min(SLC_TOP, ns)
    cmp_end = jnp.arange(nc) * CMP_STRIDE + CMP_BLOCK - 1
    blk_ids = jnp.arange(ns)
    key_pos = jnp.arange(l)
    key_blk = key_pos // SLC_BLOCK

    def one_block(args):
        qr, qo, ps = args
        s = jnp.einsum('bqghd,bngd->bqghn', qr, kc, precision=prec) * SCALE
        mc = (cmp_end[None, :] <= ps[:, None])[None, :, None, None, :]
        p = _masked_softmax(s, mc)
        o_c = jnp.einsum('bqghn,bngd->bqghd', p, vc, precision=prec)
        imp = _cmp_to_slc(jnp.sum(p, axis=3), ns)
        cur = ps // SLC_BLOCK
        vis = blk_ids[None, :] * SLC_BLOCK <= ps[:, None]
        forced = (blk_ids[None, :] == 0) | (blk_ids[None, :] == cur[:, None]) | (blk_ids[None, :] == cur[:, None] - 1)
        score = jnp.where(forced[None, :, None, :], jnp.inf, imp)
        score = jnp.where(vis[None, :, None, :], score, -jnp.inf)
        _, idx = lax.top_k(score, nsel)
        sel = jnp.any(idx[..., None] == blk_ids, axis=-2)
        ms = sel[..., key_blk] & (key_pos[None, :] <= ps[:, None])[None, :, None, :]
        ss = jnp.einsum('bqghd,bkgd->bqghk', qo, ks_rows, precision=prec) * SCALE
        pss = _masked_softmax(ss, ms[:, :, :, None, :])
        o_s = jnp.einsum('bqghk,bkgd->bqghd', pss, vs_rows, precision=prec)
        return o_c, o_s

    qb = math.gcd(t, QBLOCK)
    nqb = t // qb

    def split_q(a):
        return a.reshape(b, nqb, qb, NSA_KV, NSA_HG, HD).transpose(1, 0, 2, 3, 4, 5)

    def merge_q(a):
        return a.transpose(1, 0, 2, 3, 4, 5).reshape(b, t, NSA_HEADS, HD)

    o_c, o_s = lax.map(one_block, (split_q(q), split_q(q_rot), pos.reshape(nqb, qb)))
    return merge_q(o_c), merge_q(o_s)


def _nsa_window_banded(q_rot, win_rows, pos):
    b, t = q_rot.shape[:2]
    qb = math.gcd(t, QBLOCK)
    nb = t // qb
    span = qb + WINDOW
    rows = jnp.pad(win_rows, ((0, 0), (WINDOW, 0), (0, 0), (0, 0), (0, 0)))
    idx = jnp.arange(nb)[:, None] * qb + jnp.arange(span)[None, :]
    kv = rows[:, idx]
    kpos = idx - WINDOW
    delta = pos.reshape(nb, qb)[:, :, None] - kpos[:, None, :]
    mask = (delta >= 0) & (delta <= WINDOW) & (kpos[:, None, :] >= 0)
    q = q_rot.reshape(b, nb, qb, NSA_KV, NSA_HG, HD)
    s = jnp.einsum('bnqghd,bnkgd->bnqghk', q, kv[:, :, :, 0]) * SCALE
    p = _masked_softmax(s, mask[None, :, :, None, None, :])
    o = jnp.einsum('bnqghk,bnkgd->bnqghd', p, kv[:, :, :, 1])
    return o.reshape(b, t, NSA_HEADS, HD)


def _nsa_window_buffer(q_rot, win_all, qpos, kpos, prec):
    b, t = q_rot.shape[:2]
    q = q_rot.reshape(b, t, NSA_KV, NSA_HG, HD)
    s = jnp.einsum('bqghd,bkgd->bqghk', q, win_all[:, :, 0], precision=prec) * SCALE
    delta = qpos[:, None] - kpos[None, :]
    mask = ((delta >= 0) & (delta <= WINDOW))[None, :, None, None, :]
    p = _masked_softmax(s, mask)
    o = jnp.einsum('bqghk,bkgd->bqghd', p, win_all[:, :, 1], precision=prec)
    return o.reshape(b, t, NSA_HEADS, HD)


def _nsa_combine(q, q_rot, pos, cmp_rows, slc_rows, o_win, gate_logits, cmp_par, prec):
    b, t = q.shape[:2]
    pe_k, w1_k, w2_k, pe_v, w1_v, w2_v = cmp_par
    kc = _nsa_compress(cmp_rows[:, :, 0], pe_k, w1_k, w2_k, None)
    vc = _nsa_compress(cmp_rows[:, :, 1], pe_v, w1_v, w2_v, None)
    o_c, o_s = _nsa_cmp_slc(q, q_rot, pos, kc, vc, slc_rows[:, :, 0], slc_rows[:, :, 1], prec)
    g = jax.nn.sigmoid(gate_logits).reshape(b, t, NSA_HEADS, 3)
    o = g[..., 0:1] * o_c + g[..., 1:2] * o_s + g[..., 2:3] * o_win
    return o.reshape(b, t, NSA_W)


def _hgrn2_chunked(q, k, v, logf, s0, prec):
    b, t, h, dk = q.shape
    dv = v.shape[-1]
    c = math.gcd(t, HG_CHUNK)
    n = t // c

    def time_major(a):
        return a.reshape(b, n, c, h, a.shape[-1]).transpose(1, 0, 3, 2, 4)

    causal = jnp.arange(c)[:, None] >= jnp.arange(c)[None, :]

    def step(S, inp):
        qc, kc, vc, gc = inp
        bb = jnp.cumsum(gc, axis=2)
        inter = jnp.einsum('bhtd,bhde->bhte', qc * jnp.exp(bb), S, precision=prec)
        diff = bb[:, :, :, None, :] - bb[:, :, None, :, :]
        decay = jnp.exp(jnp.where(causal[:, :, None], diff, -jnp.inf))
        A = jnp.einsum('bhtd,bhsd,bhtsd->bhts', qc, kc, decay, precision=prec)
        o = inter + jnp.einsum('bhts,bhse->bhte', A, vc, precision=prec)
        b_last = bb[:, :, -1:, :]
        S = jnp.exp(b_last[:, :, 0, :])[..., None] * S + jnp.einsum(
            'bhsd,bhse->bhde', kc * jnp.exp(b_last - bb), vc, precision=prec)
        return S, o

    S, o = lax.scan(step, s0, (time_major(q), time_major(k), time_major(v), time_major(logf)))
    return o.transpose(1, 0, 3, 2, 4).reshape(b, t, h, dv), S


def _hgrn2_mixer(hq, hf, hv, hg, lb, g_norm, s0, prec):
    b, t = hq.shape[:2]
    q = hq.reshape(b, t, HG_HEADS, HG_DK)
    lbh = lb.reshape(HG_HEADS, HG_DK)
    f = lbh + (1.0 - lbh) * jax.nn.sigmoid(hf.reshape(b, t, HG_HEADS, HG_DK))
    logf = jnp.log(f)
    k = 1.0 - f
    v = hv.reshape(b, t, HG_HEADS, HG_DV)
    o, S = _hgrn2_chunked(q, k, v, logf, s0, prec)
    gn = g_norm.reshape(HG_HEADS, HG_DV)
    o = o * lax.rsqrt(jnp.mean(o * o, axis=-1, keepdims=True) + NORM_EPS) * gn
    o = o.reshape(b, t, HG_VW) * jax.nn.silu(hg)
    return o, S


def _split_proj(p, b, t, pos):
    q = p[:, COL_Q:COL_Q + NSA_W].reshape(b, t, NSA_HEADS, HD)
    q_rot = _rope_partial(q, pos)
    kv6 = p[:, COL_KV:COL_KV + 6 * NSA_KV * HD].reshape(b, t, 6, NSA_KV, HD)
    cmp_rows = kv6[:, :, 0:2]
    slc_rows = jnp.stack([_rope_partial(kv6[:, :, 2], pos), kv6[:, :, 3]], axis=2)
    win_rows = jnp.stack([_rope_partial(kv6[:, :, 4], pos), kv6[:, :, 5]], axis=2)
    ng = p[:, COL_NG:COL_NG + N_GATE].reshape(b, t, N_GATE)
    hq = p[:, COL_HQ:COL_HQ + HG_KW].reshape(b, t, HG_KW)
    hf = p[:, COL_HF:COL_HF + HG_KW].reshape(b, t, HG_KW)
    hv = p[:, COL_HV:COL_HV + HG_VW].reshape(b, t, HG_VW)
    hg = p[:, COL_HGATE:COL_HGATE + HG_VW].reshape(b, t, HG_VW)
    return q, q_rot, cmp_rows, slc_rows, win_rows, ng, (hq, hf, hv, hg)


def _paged_rows(pool, page_table):
    rows = pool[page_table]
    db, n_pages = page_table.shape
    return rows.reshape((db, n_pages * PAGE_SIZE) + rows.shape[3:])


def _reorder_w_in(w):
    o_q, o_kv = 0, NSA_W
    o_ng = o_kv + 6 * NSA_KV * HD
    o_hq = o_ng + N_GATE
    o_ga = o_hq + 2 * HG_KW + 2 * HG_VW
    pad = jnp.zeros((w.shape[0], D_IN_PAD - COL_NG - N_GATE), w.dtype)
    return jnp.concatenate([w[:, o_ga:o_ga + 2 * D_MODEL], w[:, o_q:o_ng], w[:, o_hq:o_ga],
                            w[:, o_ng:o_hq], pad], axis=1)


def _moe_group(top_idx, n_tok):
    n_assign = n_tok * TOP_K
    n_blk = -(-n_assign // MOE_ROWS) + N_EXPERTS
    n_rows = n_blk * MOE_ROWS
    e_flat = top_idx.reshape(-1)
    onehot = (e_flat[:, None] == jnp.arange(N_EXPERTS, dtype=jnp.int32)[None, :]).astype(jnp.int32)
    csum = jnp.cumsum(onehot, axis=0)
    rank = jnp.sum((csum - onehot) * onehot, axis=1)
    counts = csum[-1]
    padded = (counts + MOE_ROWS - 1) // MOE_ROWS * MOE_ROWS
    pend = jnp.cumsum(padded)
    pstart = pend - padded
    dest = pstart[e_flat] + rank
    tok_flat = jnp.arange(n_assign, dtype=jnp.int32) // TOP_K
    row_tok = jnp.full((n_rows,), n_tok, jnp.int32).at[dest].set(tok_flat)
    blk_e = jnp.minimum(jnp.sum(jnp.arange(n_blk)[:, None] * MOE_ROWS >= pend[None, :], axis=1), N_EXPERTS - 1)
    nused = (pend[-1] // MOE_ROWS).reshape(1)
    return row_tok, dest.reshape(n_tok, TOP_K), blk_e.astype(jnp.int32), nused.astype(jnp.int32)


def kernel(x_prompt, x_sample, cache_cmp_kv, cache_slc_kv, cache_win_kv, state_hgrn, cache_mem_kv, page_table, mem_prompt, norm_mix, w_in, cmp_pe_k, cmp_w1_k, cmp_w2_k, cmp_pe_v, cmp_w1_v, cmp_w2_v, lb_logits, hg_norm, w_branch_a, w_branch_b, w_out, norm_x, wq_x, wk_x, wv_x, wo_x, norm_ffn, w_router, b_router, w_gu, b_gu, w_dn, b_dn, norm_final):
    bsz, seq, d = x_prompt.shape
    dbs, dec_seq, _ = x_sample.shape
    past = page_table.shape[1] * PAGE_SIZE
    win_buf_len = cache_win_kv.shape[2]
    n_p = bsz * seq
    n_s = dbs * dec_seq
    pos_p = jnp.arange(seq, dtype=jnp.int32)
    pos_s = past + jnp.arange(dec_seq, dtype=jnp.int32)
    kpos_w = jnp.arange(past - win_buf_len, past + dec_seq, dtype=jnp.int32)
    lb_all = jnp.cumsum(jax.nn.softmax(lb_logits, axis=0), axis=0)
    l = 0
    cmp_par = (cmp_pe_k[l], cmp_w1_k[l], cmp_w2_k[l], cmp_pe_v[l], cmp_w1_v[l], cmp_w2_v[l])
    w_in_r = _reorder_w_in(w_in[l])
    wa, wb, wo = w_branch_a[l], w_branch_b[l], w_out[l]
    xp = x_prompt.reshape(n_p, d)
    xs = x_sample.reshape(n_s, d)

    proj_p = _norm_matmul(xp, norm_mix[l], w_in_r.astype(BF16), normalize=True, hp=False, tm=1024, tn=PROJ_TN)
    o_nsa, cmp_rows, slc_rows, win_rows = _nsa_prompt(proj_p, bsz, seq, pos_p, cmp_par)
    o_hg, s_final = _hgrn2_prompt(proj_p, bsz, seq, lb_all[l], hg_norm[l])
    xp1 = _merge(xp, o_nsa.reshape(n_p, NSA_W), o_hg, proj_p,
                 wa.astype(BF16), wb.astype(BF16), wo.astype(BF16), hp=False, tm=256)

    wkv = jnp.concatenate([wk_x[l], wv_x[l]], axis=1).astype(BF16)
    mem2d = mem_prompt.reshape(bsz * mem_prompt.shape[1], d)
    mkv = _norm_matmul(mem2d, norm_x[l], wkv, normalize=False, hp=False, tm=mem2d.shape[0], tn=PROJ_TN)
    mkv = mkv.reshape(bsz, mem_prompt.shape[1], 2 * X_W)
    xp2 = _cross(xp1.reshape(bsz, seq, d), norm_x[l], mkv, wq_x[l].astype(BF16), wo_x[l].astype(BF16), tm=512)
    xp2 = xp2.reshape(n_p, d)

    proj_s = _norm_matmul(xs, norm_mix[l], w_in_r, normalize=True, hp=True, tm=n_s, tn=PROJ_TN)
    q, q_rot, cmp_new, slc_new, win_new, g_nsa, hg_in = _split_proj(proj_s, dbs, dec_seq, pos_s)
    cmp_all = jnp.concatenate([_paged_rows(cache_cmp_kv[l], page_table), cmp_new], axis=1)
    slc_all = jnp.concatenate([_paged_rows(cache_slc_kv[l], page_table), slc_new], axis=1)
    win_all = jnp.concatenate([cache_win_kv[l], win_new], axis=1)
    o_win = _nsa_window_buffer(q_rot, win_all, pos_s, kpos_w, HI)
    o_nsa_s = _nsa_combine(q, q_rot, pos_s, cmp_all, slc_all, o_win, g_nsa, cmp_par, HI)
    o_hg_s, s_new = _hgrn2_mixer(*hg_in, lb_all[l], hg_norm[l], state_hgrn[l], HI)
    xs1 = _merge(xs, o_nsa_s.reshape(n_s, NSA_W), o_hg_s.reshape(n_s, HG_VW), proj_s, wa, wb, wo, hp=True, tm=n_s)

    hs = _rms(xs1, norm_x[l]).reshape(dbs, dec_seq, d)
    mem_s = cache_mem_kv[l]
    qx = jnp.einsum('btd,dw->btw', hs, wq_x[l], precision=HI).reshape(dbs, dec_seq, X_HEADS, X_HD)
    sx = jnp.einsum('bthd,bmhd->bhtm', qx, mem_s[:, :, 0], precision=HI) * X_SCALE
    px = jax.nn.softmax(sx, axis=-1)
    ox = jnp.einsum('bhtm,bmhd->bthd', px, mem_s[:, :, 1], precision=HI).reshape(n_s, X_W)
    xs2 = xs1 + jnp.dot(ox, wo_x[l], precision=HI)

    n_tok = n_p + n_s
    w_r_pad = jnp.pad(w_router[l], ((0, 0), (0, LANES - N_EXPERTS)))
    b_r_pad = jnp.pad(b_router[l], (0, LANES - N_EXPERTS)).reshape(1, LANES)
    h_p, logit_p = _router(xp2, norm_ffn[l], w_r_pad, b_r_pad, tm=512)
    h_s, logit_s = _router(xs2, norm_ffn[l], w_r_pad, b_r_pad, tm=n_s)
    logits = jnp.concatenate([logit_p, logit_s], axis=0)[:, :N_EXPERTS]
    top_val, top_idx = lax.top_k(logits, TOP_K)
    gates = jax.nn.softmax(top_val, axis=-1)
    row_tok, pos, blk_e, nused = _moe_group(top_idx, n_tok)
    h_all = jnp.concatenate([h_p, h_s, jnp.zeros((1, d), BF16)], axis=0)
    xb = h_all[row_tok]
    yb = _moe_experts(xb, blk_e, nused, w_gu[l], b_gu[l], w_dn[l], b_dn[l])
    yg = yb[pos.T]
    y_prompt = _combine(xp2, yg, gates, norm_final, tm=256, row0=0)
    y_sample = _combine(xs2, yg, gates, norm_final, tm=n_s, row0=n_p)

    mkv_out = mkv.reshape(bsz, mem_prompt.shape[1], 2, X_HEADS, X_HD)
    return (y_prompt.reshape(bsz, seq, d), y_sample.reshape(dbs, dec_seq, d),
            cmp_rows[None], slc_rows[None], win_rows[:, seq - min(WINDOW, seq):][None], s_final[None],
            mkv_out[None], cmp_new[None], slc_new[None],
            win_all[:, win_all.shape[1] - win_buf_len:][None], s_new[None])
```

```python
import functools
import math

import jax
import jax.numpy as jnp
from jax import lax
from jax.experimental import pallas as pl
from jax.experimental.pallas import tpu as pltpu

F32 = jnp.float32
BF16 = jnp.bfloat16
HI = lax.Precision.HIGHEST

D_MODEL = 2048
PAGE_SIZE = 128
NSA_HEADS = 8
NSA_KV = 2
NSA_HG = NSA_HEADS // NSA_KV
HD = 128
CMP_BLOCK = 32
CMP_STRIDE = 16
SLC_BLOCK = 64
SLC_TOP = 16
WINDOW = 512
QBLOCK = 128
ROPE_DIM = HD // 4
ROPE_THETA = 500000.0
SCALE = HD ** -0.5
HG_HEADS = 8
HG_DK = 128
HG_DV = 128
HG_CHUNK = 64
X_HEADS = 4
X_HD = 128
X_W = X_HEADS * X_HD
X_SCALE = X_HD ** -0.5
N_EXPERTS = 32
TOP_K = 4
D_FF = D_MODEL
SWIGLU_LIMIT = 7.0
SWIGLU_ALPHA = 1.702
NORM_EPS = 1e-5
NSA_W = NSA_HEADS * HD
HG_KW = HG_HEADS * HG_DK
HG_VW = HG_HEADS * HG_DV

LANES = 128
MXU_N = 256
VMEM_LIMIT = 52 * 1024 * 1024

COL_GA = 0
COL_GB = COL_GA + D_MODEL
COL_Q = COL_GB + D_MODEL
COL_KV = COL_Q + NSA_W
COL_HQ = COL_KV + 6 * NSA_KV * HD
COL_HF = COL_HQ + HG_KW
COL_HV = COL_HF + HG_KW
COL_HGATE = COL_HV + HG_VW
COL_NG = COL_HGATE + HG_VW
N_GATE = NSA_HEADS * 3
PROJ_TN = 512
D_IN_PAD = -(-(COL_NG + N_GATE) // PROJ_TN) * PROJ_TN

MOE_ROWS = 256
MOE_TN = 1024


def _params(sem):
    return pltpu.CompilerParams(dimension_semantics=sem, vmem_limit_bytes=VMEM_LIMIT)


def _rms(x, g):
    return x * lax.rsqrt(jnp.mean(x * x, axis=-1, keepdims=True) + NORM_EPS) * g


def _dot(a, b, hp):
    if hp:
        return jnp.dot(a.astype(F32), b.astype(F32), precision=HI, preferred_element_type=F32)
    return jnp.dot(a.astype(BF16), b.astype(BF16), preferred_element_type=F32)


def _norm_matmul_kernel(x_ref, g_ref, w_ref, o_ref, h_ref, *, normalize, hp):
    @pl.when(pl.program_id(1) == 0)
    def _():
        x = x_ref[...]
        if normalize:
            x = _rms(x, g_ref[...])
        h_ref[...] = x.astype(h_ref.dtype)

    o_ref[...] = _dot(h_ref[...], w_ref[...], hp)


def _norm_matmul(x, g, w, *, normalize, hp, tm, tn):
    m, k = x.shape
    n = w.shape[1]
    return pl.pallas_call(
        functools.partial(_norm_matmul_kernel, normalize=normalize, hp=hp),
        out_shape=jax.ShapeDtypeStruct((m, n), F32),
        grid=(m // tm, n // tn),
        in_specs=[pl.BlockSpec((tm, k), lambda i, j: (i, 0)),
                  pl.BlockSpec((1, k), lambda i, j: (0, 0)),
                  pl.BlockSpec((k, tn), lambda i, j: (0, j))],
        out_specs=pl.BlockSpec((tm, tn), lambda i, j: (i, j)),
        scratch_shapes=[pltpu.VMEM((tm, k), F32 if hp else BF16)],
        compiler_params=_params(("parallel", "arbitrary")),
    )(x, g.reshape(1, k), w)


def _merge_kernel(x_ref, on_ref, oh_ref, ga_ref, gb_ref, wa_ref, wb_ref, wo_ref, o_ref, *, hp):
    a = _dot(on_ref[...], wa_ref[...], hp)
    b = _dot(oh_ref[...], wb_ref[...], hp)
    m = jax.nn.sigmoid(ga_ref[...]) * a + jax.nn.sigmoid(gb_ref[...]) * b
    o_ref[...] = x_ref[...] + _dot(m, wo_ref[...], hp)


def _merge(x, o_nsa, o_hg, proj, wa, wb, wo, *, hp, tm):
    m = x.shape[0]
    d = D_MODEL
    const = lambda i: (0, 0)
    return pl.pallas_call(
        functools.partial(_merge_kernel, hp=hp),
        out_shape=jax.ShapeDtypeStruct((m, d), F32),
        grid=(m // tm,),
        in_specs=[pl.BlockSpec((tm, d), lambda i: (i, 0)),
                  pl.BlockSpec((tm, NSA_W), lambda i: (i, 0)),
                  pl.BlockSpec((tm, HG_VW), lambda i: (i, 0)),
                  pl.BlockSpec((tm, d), lambda i: (i, COL_GA // d)),
                  pl.BlockSpec((tm, d), lambda i: (i, COL_GB // d)),
                  pl.BlockSpec((NSA_W, d), const, pipeline_mode=pl.Buffered(1)),
                  pl.BlockSpec((HG_VW, d), const, pipeline_mode=pl.Buffered(1)),
                  pl.BlockSpec((d, d), const, pipeline_mode=pl.Buffered(1))],
        out_specs=pl.BlockSpec((tm, d), lambda i: (i, 0)),
        compiler_params=_params(("parallel",)),
    )(x, o_nsa, o_hg, proj, proj, wa, wb, wo)


def _cross_kernel(x_ref, g_ref, k_ref, v_ref, wq_ref, wo_ref, o_ref):
    x = x_ref[0]
    h = _rms(x, g_ref[...])
    q = _dot(h, wq_ref[...], False)
    k = k_ref[0].astype(BF16)
    v = v_ref[0].astype(BF16)
    outs = []
    for hd in range(X_HEADS):
        sl = slice(hd * X_HD, (hd + 1) * X_HD)
        s = lax.dot_general(q[:, sl].astype(BF16), k[:, sl], (((1,), (1,)), ((), ())),
                            preferred_element_type=F32) * X_SCALE
        s = s - jnp.max(s, axis=-1, keepdims=True)
        e = jnp.exp(s)
        p = e / jnp.sum(e, axis=-1, keepdims=True)
        outs.append(jnp.dot(p.astype(BF16), v[:, sl], preferred_element_type=F32))
    o = jnp.concatenate(outs, axis=-1)
    o_ref[0] = x + _dot(o, wo_ref[...], False)


def _cross(x, g, mkv, wq, wo, *, tm):
    b, t, d = x.shape
    mlen = mkv.shape[1]
    const = lambda bi, i: (0, 0)
    return pl.pallas_call(
        _cross_kernel,
        out_shape=jax.ShapeDtypeStruct((b, t, d), F32),
        grid=(b, t // tm),
        in_specs=[pl.BlockSpec((1, tm, d), lambda bi, i: (bi, i, 0)),
                  pl.BlockSpec((1, d), const),
                  pl.BlockSpec((1, mlen, X_W), lambda bi, i: (bi, 0, 0)),
                  pl.BlockSpec((1, mlen, X_W), lambda bi, i: (bi, 0, 1)),
                  pl.BlockSpec((d, X_W), const),
                  pl.BlockSpec((X_W, d), const)],
        out_specs=pl.BlockSpec((1, tm, d), lambda bi, i: (bi, i, 0)),
        compiler_params=_params(("parallel", "parallel")),
    )(x, g.reshape(1, d), mkv, mkv, wq, wo)


def _router_kernel(x_ref, g_ref, w_ref, b_ref, h_ref, l_ref):
    h = _rms(x_ref[...], g_ref[...])
    h_ref[...] = h.astype(BF16)
    l_ref[...] = _dot(h, w_ref[...], True) + b_ref[...]


def _router(x, g, w_r_pad, b_r_pad, *, tm):
    m, d = x.shape
    const = lambda i: (0, 0)
    return pl.pallas_call(
        _router_kernel,
        out_shape=(jax.ShapeDtypeStruct((m, d), BF16), jax.ShapeDtypeStruct((m, LANES), F32)),
        grid=(m // tm,),
        in_specs=[pl.BlockSpec((tm, d), lambda i: (i, 0)),
                  pl.BlockSpec((1, d), const),
                  pl.BlockSpec((d, LANES), const),
                  pl.BlockSpec((1, LANES), const)],
        out_specs=(pl.BlockSpec((tm, d), lambda i: (i, 0)), pl.BlockSpec((tm, LANES), lambda i: (i, 0))),
        compiler_params=_params(("parallel",)),
    )(x, g.reshape(1, d), w_r_pad, b_r_pad)


def _new_weight_tile(blk_e_ref, i):
    prev = blk_e_ref[jnp.maximum(i - 1, 0)]
    return (i == 0) | (blk_e_ref[i] != prev)


def _moe_gu_kernel(blk_e_ref, nused_ref, x_ref, w_ref, b_ref, sel_ref, o_ref, wbf_ref):
    i = pl.program_id(1)
    tn = w_ref.shape[2]

    @pl.when(_new_weight_tile(blk_e_ref, i))
    def _():
        wbf_ref[...] = w_ref[0].astype(BF16)

    @pl.when(i < nused_ref[0])
    def _():
        gu = jnp.dot(x_ref[...], wbf_ref[...], preferred_element_type=F32) + b_ref[0]
        gate = jnp.minimum(gu, SWIGLU_LIMIT)
        up = jnp.clip(gu, -SWIGLU_LIMIT, SWIGLU_LIMIT)
        up_next = pltpu.roll(up, tn - 1, axis=1)
        act = (up_next + 1.0) * gate * jax.nn.sigmoid(gate * SWIGLU_ALPHA)
        lane = lax.broadcasted_iota(jnp.int32, act.shape, 1)
        act = jnp.where(lane % 2 == 0, act, 0.0).astype(BF16)
        for c in range(tn // MXU_N):
            o_ref[:, c * LANES:(c + 1) * LANES] = jnp.dot(
                act[:, c * MXU_N:(c + 1) * MXU_N], sel_ref[...], preferred_element_type=F32).astype(BF16)

    @pl.when(i >= nused_ref[0])
    def _():
        o_ref[...] = jnp.zeros_like(o_ref)


def _moe_dn_kernel(blk_e_ref, nused_ref, a_ref, w_ref, b_ref, o_ref, wbf_ref):
    i = pl.program_id(1)

    @pl.when(_new_weight_tile(blk_e_ref, i))
    def _():
        wbf_ref[...] = w_ref[0].astype(BF16)

    @pl.when(i < nused_ref[0])
    def _():
        y = jnp.dot(a_ref[...], wbf_ref[...], preferred_element_type=F32) + b_ref[0]
        o_ref[...] = y.astype(o_ref.dtype)

    @pl.when(i >= nused_ref[0])
    def _():
        o_ref[...] = jnp.zeros_like(o_ref)


def _moe_experts(xb, blk_e, nused, w_gu, b_gu, w_dn, b_dn):
    n_rows, d = xb.shape
    n_blk = n_rows // MOE_ROWS
    tn = MOE_TN
    sel = (jnp.arange(MXU_N)[:, None] == 2 * jnp.arange(LANES)[None, :]).astype(BF16)
    act = pl.pallas_call(
        _moe_gu_kernel,
        out_shape=jax.ShapeDtypeStruct((n_rows, D_FF), BF16),
        grid_spec=pltpu.PrefetchScalarGridSpec(
            num_scalar_prefetch=2,
            grid=(2 * D_FF // tn, n_blk),
            in_specs=[pl.BlockSpec((MOE_ROWS, d), lambda j, i, be, nu: (i, 0)),
                      pl.BlockSpec((1, d, tn), lambda j, i, be, nu: (be[i], 0, j)),
                      pl.BlockSpec((1, 1, tn), lambda j, i, be, nu: (be[i], 0, j)),
                      pl.BlockSpec((MXU_N, LANES), lambda j, i, be, nu: (0, 0))],
            out_specs=pl.BlockSpec((MOE_ROWS, tn // 2), lambda j, i, be, nu: (i, j)),
            scratch_shapes=[pltpu.VMEM((d, tn), BF16)]),
        compiler_params=_params(("arbitrary", "arbitrary")),
    )(blk_e, nused, xb, w_gu, b_gu.reshape(N_EXPERTS, 1, 2 * D_FF), sel)
    return pl.pallas_call(
        _moe_dn_kernel,
        out_shape=jax.ShapeDtypeStruct((n_rows, d), BF16),
        grid_spec=pltpu.PrefetchScalarGridSpec(
            num_scalar_prefetch=2,
            grid=(d // tn, n_blk),
            in_specs=[pl.BlockSpec((MOE_ROWS, D_FF), lambda j, i, be, nu: (i, 0)),
                      pl.BlockSpec((1, D_FF, tn), lambda j, i, be, nu: (be[i], 0, j)),
                      pl.BlockSpec((1, 1, tn), lambda j, i, be, nu: (be[i], 0, j))],
            out_specs=pl.BlockSpec((MOE_ROWS, tn), lambda j, i, be, nu: (i, j)),
            scratch_shapes=[pltpu.VMEM((D_FF, tn), BF16)]),
        compiler_params=_params(("arbitrary", "arbitrary")),
    )(blk_e, nused, act, w_dn, b_dn.reshape(N_EXPERTS, 1, d))


def _combine_kernel(x_ref, y_ref, gt_ref, g_ref, o_ref):
    acc = x_ref[...]
    gt = gt_ref[...]
    for k in range(TOP_K):
        acc = acc + y_ref[k].astype(F32) * gt[:, k:k + 1]
    o_ref[...] = _rms(acc, g_ref[...])


def _combine(x, yg, gates, g, *, tm, row0):
    m, d = x.shape
    off = row0 // tm
    return pl.pallas_call(
        _combine_kernel,
        out_shape=jax.ShapeDtypeStruct((m, d), F32),
        grid=(m // tm,),
        in_specs=[pl.BlockSpec((tm, d), lambda i: (i, 0)),
                  pl.BlockSpec((TOP_K, tm, d), lambda i: (0, i + off, 0)),
                  pl.BlockSpec((tm, TOP_K), lambda i: (i + off, 0)),
                  pl.BlockSpec((1, d), lambda i: (0, 0))],
        out_specs=pl.BlockSpec((tm, d), lambda i: (i, 0)),
        compiler_params=_params(("parallel",)),
    )(x, yg, gates, g.reshape(1, d))


def _rope_tile(x, cos_t, sin_lo, sin_hi):
    half = ROPE_DIM // 2
    return x * cos_t + pltpu.roll(x, half, axis=1) * sin_hi + pltpu.roll(x, HD - half, axis=1) * sin_lo


def _nsa_prep_kernel(q_ref, cmp_ref, slc_ref, win_ref, cos_ref, slo_ref, shi_ref,
                     cmp_o, slc_o, win_o, qtc_o, qtr_o, ks_o, vst_o, kw_o, vwt_o):
    cos_t, sin_lo, sin_hi = cos_ref[...], slo_ref[...], shi_ref[...]
    cmp_o[...] = cmp_ref[...]
    for g in range(NSA_KV):
        for h in range(NSA_HG):
            col = (g * NSA_HG + h) * HD
            qh = q_ref[:, col:col + HD]
            qtc_o[0, g, 0, :, h * HD:(h + 1) * HD] = qh.T.astype(BF16)
            qtr_o[0, g, 0, :, h * HD:(h + 1) * HD] = _rope_tile(qh, cos_t, sin_lo, sin_hi).T.astype(BF16)
        kcol = slice(g * HD, (g + 1) * HD)
        vcol = slice((NSA_KV + g) * HD, (NSA_KV + g + 1) * HD)
        for src, rows_o, k_o, vt_o in ((slc_ref, slc_o, ks_o, vst_o), (win_ref, win_o, kw_o, vwt_o)):
            k_rot = _rope_tile(src[:, kcol], cos_t, sin_lo, sin_hi)
            v = src[:, vcol]
            rows_o[:, kcol] = k_rot
            rows_o[:, vcol] = v
            k_o[0, g, 0] = k_rot.astype(BF16)
            vt_o[0, g, 0] = v.T.astype(BF16)


def _rope_tables(pos):
    half = ROPE_DIM // 2
    inv = ROPE_THETA ** (-jnp.arange(0, ROPE_DIM, 2, dtype=F32) / ROPE_DIM)
    ang = pos.astype(F32)[:, None] * inv[None, :]
    cos, sin = jnp.cos(ang), jnp.sin(ang)
    t = pos.shape[0]
    ones = jnp.ones((t, HD - ROPE_DIM), F32)
    zeros = jnp.zeros((t, HD - ROPE_DIM), F32)
    zh = jnp.zeros((t, half), F32)
    cos_t = jnp.concatenate([cos, cos, ones], axis=1)
    sin_lo = jnp.concatenate([-sin, zh, zeros], axis=1)
    sin_hi = jnp.concatenate([zh, sin, zeros], axis=1)
    return cos_t, sin_lo, sin_hi


def _nsa_prep(proj, bsz, seq, pos):
    n = bsz * seq
    tq = QBLOCK
    nq = seq // tq
    kvw = 2 * NSA_KV * HD
    cos_t, sin_lo, sin_hi = _rope_tables(pos)
    row = lambda r: (r, 0)
    tab = lambda r: (r % nq, 0)
    blk5 = lambda r: (r // nq, 0, r % nq, 0, 0)
    assert nq % SLC_TILES == 0
    big_k = lambda r: (r // nq, 0, (r % nq) // SLC_TILES, (r % nq) % SLC_TILES, 0)
    big_vt = lambda r: (r // nq, 0, (r % nq) // SLC_TILES, 0, (r % nq) % SLC_TILES)
    qt_shape = jax.ShapeDtypeStruct((bsz, NSA_KV, nq, HD, NSA_HG * tq), BF16)
    kv_shape = jax.ShapeDtypeStruct((bsz, NSA_KV, nq, tq, HD), BF16)
    ks_shape = jax.ShapeDtypeStruct((bsz, NSA_KV, nq // SLC_TILES, SLC_TILES * tq, HD), BF16)
    vst_shape = jax.ShapeDtypeStruct((bsz, NSA_KV, nq // SLC_TILES, HD, SLC_TILES * tq), BF16)
    rows_shape = jax.ShapeDtypeStruct((n, kvw), F32)
    return pl.pallas_call(
        _nsa_prep_kernel,
        out_shape=(rows_shape, rows_shape, rows_shape, qt_shape, qt_shape, ks_shape, vst_shape, kv_shape, kv_shape),
        grid=(n // tq,),
        in_specs=[pl.BlockSpec((tq, NSA_W), lambda r: (r, COL_Q // NSA_W)),
                  pl.BlockSpec((tq, kvw), lambda r: (r, COL_KV // kvw)),
                  pl.BlockSpec((tq, kvw), lambda r: (r, COL_KV // kvw + 1)),
                  pl.BlockSpec((tq, kvw), lambda r: (r, COL_KV // kvw + 2)),
                  pl.BlockSpec((tq, HD), tab), pl.BlockSpec((tq, HD), tab), pl.BlockSpec((tq, HD), tab)],
        out_specs=(pl.BlockSpec((tq, kvw), row), pl.BlockSpec((tq, kvw), row), pl.BlockSpec((tq, kvw), row),
                   pl.BlockSpec((1, NSA_KV, 1, HD, NSA_HG * tq), blk5),
                   pl.BlockSpec((1, NSA_KV, 1, HD, NSA_HG * tq), blk5),
                   pl.BlockSpec((1, NSA_KV, 1, tq, HD), big_k), pl.BlockSpec((1, NSA_KV, 1, HD, tq), big_vt),
                   pl.BlockSpec((1, NSA_KV, 1, tq, HD), blk5), pl.BlockSpec((1, NSA_KV, 1, HD, tq), blk5)),
        compiler_params=_params(("parallel",)),
    )(proj, proj, proj, proj, cos_t, sin_lo, sin_hi)


def _compress_kernel(x_ref, pe_ref, w1_ref, w2_ref, nat_o, tr_o):
    nch = x_ref.shape[1] // CMP_STRIDE
    acc = jnp.zeros((nch, 2 * HD), F32)
    bias = [jnp.zeros((8, HD), F32) for _ in range(CMP_BLOCK // CMP_STRIDE)]
    for s in range(CMP_STRIDE):
        w = w1_ref[0, s]
        rows = x_ref[0, pl.ds(s, nch, stride=CMP_STRIDE), :]
        acc = acc + jnp.dot(rows.astype(BF16), w, preferred_element_type=F32)
        for r in range(CMP_BLOCK // CMP_STRIDE):
            pe_row = jnp.broadcast_to(pe_ref[0, r * CMP_STRIDE + s:r * CMP_STRIDE + s + 1, :], (8, HD))
            pb = jnp.dot(pe_row.astype(BF16), w, preferred_element_type=F32)
            bias[r] = bias[r] + pb[:, r * HD:(r + 1) * HD]
    first = acc[:, :HD] + bias[0][0:1]
    second = acc[:, HD:] + bias[1][0:1]
    pre = first + pltpu.roll(second, nch - 1, axis=0)
    out = jnp.dot(jax.nn.silu(pre).astype(BF16), w2_ref[0], preferred_element_type=F32)
    row = lax.broadcasted_iota(jnp.int32, out.shape, 0)
    out = jnp.where(row < nch - 1, out, 0.0)
    nat_o[0, 0] = out.astype(BF16)
    tr_o[0, 0] = out.T.astype(BF16)


def _compress_weights(cmp_par):
    pe_k, w1_k, w2_k, pe_v, w1_v, w2_v = cmp_par

    def cat(w1):
        return jnp.concatenate([w1[:CMP_STRIDE], w1[CMP_STRIDE:]], axis=2)

    pe = jnp.stack([pe_k, pe_v])
    w1 = jnp.stack([cat(w1_k), cat(w1_v)]).astype(BF16)
    w2 = jnp.stack([w2_k, w2_v]).astype(BF16)
    return pe, w1, w2


def _nsa_compress_rows(rows, cmp_par):
    bsz, length, _ = rows.shape
    nch = length // CMP_STRIDE
    ncol = 2 * NSA_KV
    pe, w1, w2 = _compress_weights(cmp_par)
    return pl.pallas_call(
        _compress_kernel,
        out_shape=(jax.ShapeDtypeStruct((bsz, ncol, nch, HD), BF16), jax.ShapeDtypeStruct((bsz, ncol, HD, nch), BF16)),
        grid=(bsz, ncol),
        in_specs=[pl.BlockSpec((1, length, HD), lambda b, c: (b, 0, c)),
                  pl.BlockSpec((1, CMP_BLOCK, HD), lambda b, c: (c // NSA_KV, 0, 0)),
                  pl.BlockSpec((1, CMP_STRIDE, HD, 2 * HD), lambda b, c: (c // NSA_KV, 0, 0, 0)),
                  pl.BlockSpec((1, HD, HD), lambda b, c: (c // NSA_KV, 0, 0))],
        out_specs=(pl.BlockSpec((1, 1, nch, HD), lambda b, c: (b, c, 0, 0)),
                   pl.BlockSpec((1, 1, HD, nch), lambda b, c: (b, c, 0, 0))),
        compiler_params=_params(("parallel", "parallel")),
    )(rows, pe, w1, w2)


def _imp_matrix_t(nch, ns):
    a = SLC_BLOCK // CMP_STRIDE
    b = CMP_BLOCK // CMP_STRIDE
    j = jnp.arange(ns)[:, None]
    n = jnp.arange(nch)[None, :]
    out = jnp.zeros((ns, nch), F32)
    for m in range(a):
        for k in range(b):
            out = out + (n == a * j + a - 1 - m - k).astype(F32)
    return out


SLC_TILES = 4


def _flash_t(k_ref, vt_ref, qt, acc_ref, hi, bias_fn):
    width = qt.shape[1]
    acc_ref[...] = jnp.zeros_like(acc_ref)

    def body(it, carry):
        m_old, l_old = carry
        s = jnp.dot(k_ref[0, 0, it], qt, preferred_element_type=F32) * SCALE + bias_fn(it)
        m_new = jnp.maximum(m_old, jnp.max(s, axis=0, keepdims=True))
        m_safe = jnp.where(m_new == -jnp.inf, 0.0, m_new)
        alpha = jnp.exp(m_old - m_safe)
        p = jnp.exp(s - m_safe)
        l_new = alpha * l_old + jnp.sum(p, axis=0, keepdims=True)
        acc_ref[...] = alpha * acc_ref[...] + jnp.dot(vt_ref[0, 0, it], p.astype(BF16), preferred_element_type=F32)
        return m_new, l_new

    init = (jnp.full((1, width), -jnp.inf, F32), jnp.zeros((1, width), F32))
    _, l_fin = lax.fori_loop(0, hi, body, init)
    return acc_ref[...] / jnp.where(l_fin > 0, l_fin, 1.0)


def _nsa_attn_kernel(qtc_ref, qtr_ref, kc_ref, vct_ref, imp_ref, ks_ref, vst_ref, kw_ref, vwt_ref, ng_ref,
                     o_ref, acc_ref, sel_ref):
    qi = pl.program_id(2)
    tq = QBLOCK
    nch = kc_ref.shape[2]
    ns = imp_ref.shape[0]
    tile4 = lambda a: jnp.concatenate([a] * NSA_HG, axis=1)
    pos = qi * tq + lax.broadcasted_iota(jnp.int32, (1, tq), 1)

    s = jnp.dot(kc_ref[0, 0], qtc_ref[0, 0, 0], preferred_element_type=F32) * SCALE
    cmp_end = lax.broadcasted_iota(jnp.int32, (nch, tq), 0) * CMP_STRIDE + (CMP_BLOCK - 1)
    mask_c = tile4(cmp_end <= pos)
    s = jnp.where(mask_c, s, -jnp.inf)
    m = jnp.max(s, axis=0, keepdims=True)
    m = jnp.where(m == -jnp.inf, 0.0, m)
    e = jnp.exp(s - m)
    d = jnp.sum(e, axis=0, keepdims=True)
    p = e / jnp.where(d > 0, d, 1.0)
    o_c = jnp.dot(vct_ref[0, 0], p.astype(BF16), preferred_element_type=F32)
    p_grp = p[:, 0:tq]
    for h in range(1, NSA_HG):
        p_grp = p_grp + p[:, h * tq:(h + 1) * tq]
    imp = jnp.dot(imp_ref[...], p_grp, precision=HI, preferred_element_type=F32)

    blk = lax.broadcasted_iota(jnp.int32, (ns, tq), 0)
    cur = pos // SLC_BLOCK
    forced = (blk == 0) | (blk == cur) | (blk == cur - 1)
    score = jnp.where(forced, jnp.inf, imp)
    score = jnp.where(blk <= cur, score, -jnp.inf)
    sel = jnp.zeros((ns, tq), F32)
    for _ in range(min(SLC_TOP, ns)):
        best = jnp.max(score, axis=0, keepdims=True)
        idx = jnp.min(jnp.where(score == best, blk, ns), axis=0, keepdims=True)
        pick = blk == idx
        sel = jnp.where(pick, 1.0, sel)
        score = jnp.where(pick, -jnp.inf, score)
    sel_ref[...] = sel

    big = SLC_TILES * tq
    per_step = big // SLC_BLOCK
    key_in_step = lax.broadcasted_iota(jnp.int32, (big, tq), 0)

    def bias_slc(it):
        chosen = jnp.concatenate([jnp.broadcast_to(sel_ref[pl.ds(it * per_step + b, 1), :], (SLC_BLOCK, tq))
                                  for b in range(per_step)], axis=0)
        ok = (chosen > 0.5) & (it * big + key_in_step <= pos)
        return tile4(jnp.where(ok, 0.0, -jnp.inf))

    qtr = qtr_ref[0, 0, 0]
    o_s = _flash_t(ks_ref, vst_ref, qtr, acc_ref, qi // SLC_TILES + 1, bias_slc)

    key_in_tile = lax.broadcasted_iota(jnp.int32, (tq, tq), 0)
    win_tiles = [qi - WINDOW // tq + j for j in range(WINDOW // tq + 1)]
    biases = []
    for kt in win_tiles:
        key_pos = kt * tq + key_in_tile
        delta = pos - key_pos
        biases.append(jnp.where((delta >= 0) & (delta <= WINDOW) & (key_pos >= 0), 0.0, -jnp.inf))
    k_win = jnp.concatenate([kw_ref[0, 0, jnp.maximum(kt, 0)] for kt in win_tiles], axis=0)
    s = jnp.dot(k_win, qtr, preferred_element_type=F32) * SCALE + tile4(jnp.concatenate(biases, axis=0))
    e = jnp.exp(s - jnp.max(s, axis=0, keepdims=True))
    d = jnp.sum(e, axis=0, keepdims=True)
    o_w = jnp.zeros_like(o_c)
    for j, kt in enumerate(win_tiles):
        o_w = o_w + jnp.dot(vwt_ref[0, 0, jnp.maximum(kt, 0)], e[j * tq:(j + 1) * tq].astype(BF16),
                            preferred_element_type=F32)
    o_w = o_w / d

    gates = jax.nn.sigmoid(ng_ref[0, 0])
    for h in range(NSA_HG):
        gate = gates[3 * h:3 * h + 3]
        sl = slice(h * tq, (h + 1) * tq)
        o = gate[0:1] * o_c[:, sl] + gate[1:2] * o_s[:, sl] + gate[2:3] * o_w[:, sl]
        o_ref[0, :, h * HD:(h + 1) * HD] = o.T.astype(o_ref.dtype)


def _nsa_attn(qtc, qtr, kc_nat, kc_tr, ks5, vst5, kw5, vwt5, ng_t):
    bsz, _, nq, _, width = qtc.shape
    tq = QBLOCK
    seq = nq * tq
    nch = kc_nat.shape[2]
    ns = seq // SLC_BLOCK
    imp_t = _imp_matrix_t(nch, ns)
    q_spec = pl.BlockSpec((1, 1, 1, HD, width), lambda b, g, i: (b, g, i, 0, 0))
    k5_spec = pl.BlockSpec((1, 1, nq, tq, HD), lambda b, g, i: (b, g, 0, 0, 0))
    whole = lambda a: pl.BlockSpec((1, 1) + a.shape[2:], lambda b, g, i: (b, g, 0, 0, 0))
    return pl.pallas_call(
        _nsa_attn_kernel,
        out_shape=jax.ShapeDtypeStruct((bsz, seq, NSA_W), BF16),
        grid=(bsz, NSA_KV, nq),
        in_specs=[q_spec, q_spec,
                  pl.BlockSpec((1, 1, nch, HD), lambda b, g, i: (b, g, 0, 0)),
                  pl.BlockSpec((1, 1, HD, nch), lambda b, g, i: (b, NSA_KV + g, 0, 0)),
                  pl.BlockSpec((ns, nch), lambda b, g, i: (0, 0)),
                  whole(ks5), whole(vst5), k5_spec, k5_spec,
                  pl.BlockSpec((1, 1, N_GATE // NSA_KV, tq), lambda b, g, i: (b, g, 0, i))],
        out_specs=pl.BlockSpec((1, tq, NSA_HG * HD), lambda b, g, i: (b, i, g)),
        scratch_shapes=[pltpu.VMEM((HD, width), F32), pltpu.VMEM((ns, tq), F32)],
        compiler_params=_params(("parallel", "parallel", "arbitrary")),
    )(qtc, qtr, kc_nat, kc_tr, imp_t, ks5, vst5, kw5, vwt5, ng_t)


def _nsa_prompt(proj, bsz, seq, pos, cmp_par):
    cmp_rows, slc_rows, win_rows, qtc, qtr, ks5, vst5, kw5, vwt5 = _nsa_prep(proj, bsz, seq, pos)
    kc_nat, kc_tr = _nsa_compress_rows(cmp_rows.reshape(bsz, seq, -1), cmp_par)
    ng_t = proj[:, COL_NG:COL_NG + N_GATE].reshape(bsz, seq, NSA_KV, N_GATE // NSA_KV).transpose(0, 2, 3, 1)
    o_nsa = _nsa_attn(qtc, qtr, kc_nat, kc_tr, ks5, vst5, kw5, vwt5, ng_t)
    shape5 = (bsz, seq, 2, NSA_KV, HD)
    return o_nsa, cmp_rows.reshape(shape5), slc_rows.reshape(shape5), win_rows.reshape(shape5)


HG_TILE = 128
HG_DIAG = 8


def _split3(x):
    hi = x.astype(BF16)
    r1 = x - hi.astype(F32)
    mid = r1.astype(BF16)
    lo = (r1 - mid.astype(F32)).astype(BF16)
    return hi, mid, lo


def _hgrn2_kernel(hq_ref, hf_ref, hv_ref, hg_ref, lb_ref, gn_ref, o_ref, s_ref, st_ref):
    i = pl.program_id(2)
    c = HG_TILE

    @pl.when(i == 0)
    def _():
        st_ref[...] = jnp.zeros_like(st_ref)

    lb = lb_ref[0]
    q = hq_ref[...]
    f = lb + (1.0 - lb) * jax.nn.sigmoid(hf_ref[...])
    logf = jnp.log(f)
    k = 1.0 - f
    v = hv_ref[...]
    row = lax.broadcasted_iota(jnp.int32, (c, c), 0)
    col = lax.broadcasted_iota(jnp.int32, (c, c), 1)
    tri = (row >= col).astype(BF16)
    b = sum(jnp.dot(tri, part, preferred_element_type=F32) for part in _split3(logf))
    b_last = b[c - 1:c, :]

    o = (jnp.sum(q * k, axis=1, keepdims=True)) * v
    for delta in range(1, HG_DIAG):
        valid = (row % HG_DIAG) >= delta
        decay = jnp.exp(jnp.where(valid, b - pltpu.roll(b, delta, axis=0), -jnp.inf))
        a = jnp.sum(q * pltpu.roll(k, delta, axis=0) * decay, axis=1, keepdims=True)
        o = o + a * pltpu.roll(v, delta, axis=0)

    a_mat = jnp.zeros((c, c), F32)
    blk = 2 * HG_DIAG
    while blk <= c:
        half = blk // 2
        b_mid = jnp.concatenate(
            [jnp.broadcast_to(b[s + half - 1:s + half, :], (blk, b.shape[1])) for s in range(0, c, blk)], axis=0)
        second = (row % blk) >= half
        qs = q * jnp.exp(jnp.where(second, b - b_mid, -jnp.inf))
        ks = k * jnp.exp(jnp.where(second, -jnp.inf, b_mid - b))
        a_blk = lax.dot_general(qs.astype(BF16), ks.astype(BF16), (((1,), (1,)), ((), ())),
                                preferred_element_type=F32)
        a_mat = a_mat + jnp.where((row // blk) == (col // blk), a_blk, 0.0)
        blk *= 2

    st = st_ref[...]
    o = o + jnp.dot(a_mat.astype(BF16), v.astype(BF16), preferred_element_type=F32)
    o = o + lax.dot_general((q * jnp.exp(b)).astype(BF16), st.astype(BF16), (((1,), (1,)), ((), ())),
                            preferred_element_type=F32)
    k_end = (k * jnp.exp(b_last - b)).astype(BF16)
    st_new = st * jnp.exp(b_last) + jnp.dot(v.T.astype(BF16), k_end, preferred_element_type=F32)
    st_ref[...] = st_new

    o = o * lax.rsqrt(jnp.mean(o * o, axis=1, keepdims=True) + NORM_EPS) * gn_ref[0]
    o_ref[...] = (o * jax.nn.silu(hg_ref[...])).astype(o_ref.dtype)

    @pl.when(i == pl.num_programs(2) - 1)
    def _():
        s_ref[0, 0] = st_new.T


def _hgrn2_prompt(proj, bsz, seq, lb, g_norm):
    c = HG_TILE
    nc = seq // c
    col = lambda base: (lambda b, h, i: (b * nc + i, base // HG_DK + h))
    vec = pl.BlockSpec((1, 1, HG_DK), lambda b, h, i: (h, 0, 0))
    return pl.pallas_call(
        _hgrn2_kernel,
        out_shape=(jax.ShapeDtypeStruct((bsz * seq, HG_VW), BF16),
                   jax.ShapeDtypeStruct((bsz, HG_HEADS, HG_DK, HG_DV), F32)),
        grid=(bsz, HG_HEADS, nc),
        in_specs=[pl.BlockSpec((c, HG_DK), col(COL_HQ)), pl.BlockSpec((c, HG_DK), col(COL_HF)),
                  pl.BlockSpec((c, HG_DV), col(COL_HV)), pl.BlockSpec((c, HG_DV), col(COL_HGATE)), vec, vec],
        out_specs=(pl.BlockSpec((c, HG_DV), lambda b, h, i: (b * nc + i, h)),
                   pl.BlockSpec((1, 1, HG_DK, HG_DV), lambda b, h, i: (b, h, 0, 0))),
        scratch_shapes=[pltpu.VMEM((HG_DV, HG_DK), F32)],
        compiler_params=_params(("parallel", "parallel", "arbitrary")),
    )(proj, proj, proj, proj, lb.reshape(HG_HEADS, 1, HG_DK), g_norm.reshape(HG_HEADS, 1, HG_DV))


PAGES_PER_STEP = 16


def _paged_compress_kernel(pt_ref, *refs):
    pages = refs[:PAGES_PER_STEP]
    w_ref = refs[PAGES_PER_STEP]
    o_ref = refs[PAGES_PER_STEP + 1]
    per_page = PAGE_SIZE // CMP_STRIDE
    ncol = 2 * NSA_KV
    for c in range(ncol):
        acc = jnp.zeros((PAGES_PER_STEP * per_page, 2 * HD), F32)
        for s2 in range(CMP_STRIDE // 2):
            def rows(s):
                return jnp.concatenate(
                    [p[0, pl.ds(s * ncol + c, per_page, stride=CMP_STRIDE * ncol), :] for p in pages], axis=0)
            lhs = jnp.concatenate([rows(2 * s2), rows(2 * s2 + 1)], axis=1).astype(BF16)
            acc = acc + jnp.dot(lhs, w_ref[c // NSA_KV, s2], preferred_element_type=F32)
        o_ref[0, c] = acc


def _paged_compress(cache, page_table, w1cat):
    dbs, n_pages = page_table.shape
    per_page = PAGE_SIZE // CMP_STRIDE
    w_st = w1cat.reshape(2, CMP_STRIDE // 2, 2 * HD, 2 * HD)
    page_spec = lambda k: pl.BlockSpec((1, PAGE_SIZE * 2 * NSA_KV, HD),
                                       lambda b, s, pt: (pt[b, s * PAGES_PER_STEP + k], 0, 0))
    return pl.pallas_call(
        _paged_compress_kernel,
        out_shape=jax.ShapeDtypeStruct((dbs, 2 * NSA_KV, n_pages * per_page, 2 * HD), F32),
        grid_spec=pltpu.PrefetchScalarGridSpec(
            num_scalar_prefetch=1,
            grid=(dbs, n_pages // PAGES_PER_STEP),
            in_specs=[page_spec(k) for k in range(PAGES_PER_STEP)]
            + [pl.BlockSpec((2, CMP_STRIDE // 2, 2 * HD, 2 * HD), lambda b, s, pt: (0, 0, 0, 0))],
            out_specs=pl.BlockSpec((1, 2 * NSA_KV, PAGES_PER_STEP * per_page, 2 * HD), lambda b, s, pt: (b, 0, s, 0))),
        compiler_params=_params(("parallel", "arbitrary")),
    )(page_table, *([cache] * PAGES_PER_STEP), w_st)


def _dec_select_kernel(p_ref, bias_ref, w2_ref, q_ref, imp_ref, oc_ref, idx_ref, *, pos):
    nch = p_ref.shape[2]
    ns_pad = imp_ref.shape[1]
    ns = pos // SLC_BLOCK + 1
    blocks = []
    for c in range(2 * NSA_KV):
        part = p_ref[0, c]
        pre = part[:, :HD] + pltpu.roll(part[:, HD:], nch - 1, axis=0) + bias_ref[c // NSA_KV]
        out = jnp.dot(jax.nn.silu(pre), w2_ref[c // NSA_KV], precision=HI, preferred_element_type=F32)
        blocks.append(out)
    q = q_ref[0]
    head_grp = lax.broadcasted_iota(jnp.int32, (NSA_HEADS, 1), 0) // NSA_HG
    n_idx = lax.broadcasted_iota(jnp.int32, (1, nch), 1)
    n_col = lax.broadcasted_iota(jnp.int32, (nch, 1), 0)
    mask_c =(n_idx * CMP_STRIDE + (CMP_BLOCK - 1) <= pos) & (n_idx < nch - 1)
    lane = lax.broadcasted_iota(jnp.int32, (NSA_HEADS, ns_pad), 1)
    out_lane = lax.broadcasted_iota(jnp.int32, (NSA_HEADS, LANES), 1)
    out_row = lax.broadcasted_iota(jnp.int32, (NSA_HEADS, LANES), 0)
    cur = pos // SLC_BLOCK
    o_c = jnp.zeros((NSA_HEADS, HD), F32)
    idx_out = jnp.zeros((NSA_HEADS, LANES), jnp.int32)
    for g in range(NSA_KV):
        s = lax.dot_general(q, blocks[g], (((1,), (1,)), ((), ())), precision=HI,
                            preferred_element_type=F32) * SCALE
        s = jnp.where(mask_c, s, -jnp.inf)
        m = jnp.max(s, axis=1, keepdims=True)
        m = jnp.where(m == -jnp.inf, 0.0, m)
        e = jnp.exp(s - m)
        d = jnp.sum(e, axis=1, keepdims=True)
        p = e / jnp.where(d > 0, d, 1.0)
        in_grp = head_grp == g
        o_g = jnp.dot(p, jnp.where(n_col < nch - 1, blocks[NSA_KV + g], 0.0), precision=HI,
                      preferred_element_type=F32)
        o_c = jnp.where(in_grp, o_g, o_c)
        p_grp = jnp.sum(jnp.where(in_grp, p, 0.0), axis=0, keepdims=True)
        imp = jnp.dot(jnp.broadcast_to(p_grp, (NSA_HEADS, nch)), imp_ref[...], precision=HI,
                      preferred_element_type=F32)
        forced = (lane == 0) | (lane == cur) | (lane == cur - 1)
        score = jnp.where(forced, jnp.inf, imp)
        score = jnp.where(lane < ns, score, -jnp.inf)
        for r in range(min(SLC_TOP, ns)):
            best = jnp.max(score, axis=1, keepdims=True)
            idx = jnp.min(jnp.where(score == best, lane, ns_pad), axis=1, keepdims=True)
            score = jnp.where(lane == idx, -jnp.inf, score)
            idx_out = jnp.where((out_lane == r) & (out_row == g), idx[:, 0:1], idx_out)
    oc_ref[0] = o_c
    idx_ref[0] = idx_out


def _dec_select(part, bias, w2, q, pos):
    dbs, ncol, nch, _ = part.shape
    ns = pos // SLC_BLOCK + 1
    ns_pad = -(-ns // LANES) * LANES
    imp = jnp.pad(_imp_matrix_t(nch, ns).T, ((0, 0), (0, ns_pad - ns)))
    return pl.pallas_call(
        functools.partial(_dec_select_kernel, pos=pos),
        out_shape=(jax.ShapeDtypeStruct((dbs, NSA_HEADS, HD), F32),
                   jax.ShapeDtypeStruct((dbs, NSA_HEADS, LANES), jnp.int32)),
        grid=(dbs,),
        in_specs=[pl.BlockSpec((1, ncol, nch, 2 * HD), lambda b: (b, 0, 0, 0)),
                  pl.BlockSpec((2, 1, HD), lambda b: (0, 0, 0)),
                  pl.BlockSpec((2, HD, HD), lambda b: (0, 0, 0)),
                  pl.BlockSpec((1, NSA_HEADS, HD), lambda b: (b, 0, 0)),
                  pl.BlockSpec((nch, ns_pad), lambda b: (0, 0))],
        out_specs=(pl.BlockSpec((1, NSA_HEADS, HD), lambda b: (b, 0, 0)),
                   pl.BlockSpec((1, NSA_HEADS, LANES), lambda b: (b, 0, 0))),
        compiler_params=_params(("parallel",)),
    )(part, bias, w2, q, imp)


def _dec_attend(q, key_tiles, val_tiles, valids, k_new, v_new):
    scores = []
    for kt, ok in zip(key_tiles, valids):
        s = lax.dot_general(q, kt, (((1,), (1,)), ((), ())), precision=HI, preferred_element_type=F32) * SCALE
        scores.append(s if ok is None else jnp.where(ok, s, -jnp.inf))
    s_new = jnp.sum(q * k_new, axis=1, keepdims=True) * SCALE
    m = s_new
    for s in scores:
        m = jnp.maximum(m, jnp.max(s, axis=1, keepdims=True))
    e_new = jnp.exp(s_new - m)
    den = e_new
    acc = e_new * v_new
    for s, vt in zip(scores, val_tiles):
        e = jnp.exp(s - m)
        den = den + jnp.sum(e, axis=1, keepdims=True)
        acc = acc + jnp.dot(e, vt, precision=HI, preferred_element_type=F32)
    return acc / den


def _dec_slc_win_kernel(pt_ref, idx_ref, *refs, last_block):
    nsel = SLC_TOP
    k_blks, v_blks = refs[:nsel], refs[nsel:2 * nsel]
    q_ref, snew_ref, kw_ref, vw_ref, wnew_ref, oc_ref, ng_ref, o_ref = refs[2 * nsel:]
    b, g = pl.program_id(0), pl.program_id(1)
    q = q_ref[0, 0]
    base = (b * NSA_KV + g) * nsel
    valids = [idx_ref[base + k] != last_block for k in range(nsel)]
    o_s = _dec_attend(q, [r[0] for r in k_blks], [r[0] for r in v_blks], valids,
                      snew_ref[0, 0, pl.ds(g, 1), :], snew_ref[0, 1, pl.ds(g, 1), :])
    o_w = _dec_attend(q, [kw_ref[0]], [vw_ref[0]], [None],
                      wnew_ref[0, 0, pl.ds(g, 1), :], wnew_ref[0, 1, pl.ds(g, 1), :])
    gate = jax.nn.sigmoid(ng_ref[0, 0])
    o_ref[0, 0] = gate[:, 0:1] * oc_ref[0, 0] + gate[:, 1:2] * o_s + gate[:, 2:3] * o_w


def _dec_slc_win(slc_cache, page_table, sel_idx, q_rot, slc_new, win_cache, win_new, o_c, ng, *, last_block):
    dbs = q_rot.shape[0]
    per_page = PAGE_SIZE // SLC_BLOCK
    halves = slc_cache.reshape(slc_cache.shape[0] * per_page, SLC_BLOCK, 4 * HD)
    wlen = win_cache.shape[1]

    def blk_spec(k, col0):
        def index(b, g, pt, idx):
            j = jnp.minimum(idx[(b * NSA_KV + g) * SLC_TOP + k], last_block - 1)
            return (pt[b * (page_table.shape[1]) + j // per_page] * per_page + j % per_page, 0, col0 + g)
        return pl.BlockSpec((1, SLC_BLOCK, HD), index)

    grp4 = pl.BlockSpec((1, 1, NSA_HG, HD), lambda b, g, pt, idx: (b, g, 0, 0))
    new4 = pl.BlockSpec((1, 2, NSA_KV, HD), lambda b, g, pt, idx: (b, 0, 0, 0))
    return pl.pallas_call(
        functools.partial(_dec_slc_win_kernel, last_block=last_block),
        out_shape=jax.ShapeDtypeStruct((dbs, NSA_KV, NSA_HG, HD), F32),
        grid_spec=pltpu.PrefetchScalarGridSpec(
            num_scalar_prefetch=2,
            grid=(dbs, NSA_KV),
            in_specs=[blk_spec(k, 0) for k in range(SLC_TOP)] + [blk_spec(k, NSA_KV) for k in range(SLC_TOP)]
            + [grp4, new4,
               pl.BlockSpec((1, wlen, HD), lambda b, g, pt, idx: (b, 0, g)),
               pl.BlockSpec((1, wlen, HD), lambda b, g, pt, idx: (b, 0, NSA_KV + g)),
               new4, grp4,
               pl.BlockSpec((1, 1, NSA_HG, 3), lambda b, g, pt, idx: (b, g, 0, 0))],
            out_specs=grp4),
        compiler_params=_params(("parallel", "parallel")),
    )(page_table.reshape(-1), sel_idx, *([halves] * (2 * SLC_TOP)), q_rot, slc_new, win_cache, win_cache, win_new,
      o_c, ng)


def _hgrn2_step_kernel(hq_ref, hf_ref, hv_ref, hg_ref, lb_ref, gn_ref, s_ref, o_ref, so_ref):
    dbs = hq_ref.shape[0]
    lb = lb_ref[0]
    f_all = lb + (1.0 - lb) * jax.nn.sigmoid(hf_ref[...])
    row = lax.broadcasted_iota(jnp.int32, (HG_DK, HG_DK), 0)
    col = lax.broadcasted_iota(jnp.int32, (HG_DK, HG_DK), 1)
    diag = row == col

    def as_col(v_row):
        return jnp.sum(jnp.where(diag, v_row, 0.0), axis=1, keepdims=True)

    for b in range(dbs):
        f = f_all[b:b + 1]
        s_new = as_col(f) * s_ref[b, 0] + as_col(1.0 - f) * hv_ref[b:b + 1, :]
        so_ref[b, 0] = s_new
        o_ref[b:b + 1, :] = jnp.sum(s_new * as_col(hq_ref[b:b + 1, :]), axis=0, keepdims=True)
    o = o_ref[...]
    o = o * lax.rsqrt(jnp.mean(o * o, axis=1, keepdims=True) + NORM_EPS) * gn_ref[0]
    o_ref[...] = o * jax.nn.silu(hg_ref[...])


def _hgrn2_step(proj, state, lb, g_norm):
    dbs = proj.shape[0]
    col = lambda base: (lambda h: (0, base // HG_DK + h))
    vec = pl.BlockSpec((1, 1, HG_DK), lambda h: (h, 0, 0))
    st = pl.BlockSpec((dbs, 1, HG_DK, HG_DV), lambda h: (0, h, 0, 0))
    return pl.pallas_call(
        _hgrn2_step_kernel,
        out_shape=(jax.ShapeDtypeStruct((dbs, HG_VW), F32), jax.ShapeDtypeStruct(state.shape, F32)),
        grid=(HG_HEADS,),
        in_specs=[pl.BlockSpec((dbs, HG_DK), col(COL_HQ)), pl.BlockSpec((dbs, HG_DK), col(COL_HF)),
                  pl.BlockSpec((dbs, HG_DV), col(COL_HV)), pl.BlockSpec((dbs, HG_DV), col(COL_HGATE)),
                  vec, vec, st],
        out_specs=(pl.BlockSpec((dbs, HG_DV), lambda h: (0, h)), st),
        compiler_params=_params(("parallel",)),
    )(proj, proj, proj, proj, lb.reshape(HG_HEADS, 1, HG_DK), g_norm.reshape(HG_HEADS, 1, HG_DV), state)


def _nsa_decode(proj, pos, cache_cmp, cache_slc, cache_win, page_table, cmp_par):
    dbs = proj.shape[0]
    n_pool = cache_cmp.shape[0]
    assert pos % SLC_BLOCK == 0 and pos % CMP_STRIDE == 0 and pos == page_table.shape[1] * PAGE_SIZE
    assert cache_win.shape[1] <= WINDOW
    pos_arr = jnp.full((1,), pos, jnp.int32)
    q = proj[:, COL_Q:COL_Q + NSA_W].reshape(dbs, 1, NSA_HEADS, HD)
    q_rot = _rope_partial(q, pos_arr)
    kv6 = proj[:, COL_KV:COL_KV + 6 * NSA_KV * HD].reshape(dbs, 1, 6, NSA_KV, HD)
    cmp_new = kv6[:, :, 0:2]
    slc_new = jnp.stack([_rope_partial(kv6[:, :, 2], pos_arr), kv6[:, :, 3]], axis=2)
    win_new = jnp.stack([_rope_partial(kv6[:, :, 4], pos_arr), kv6[:, :, 5]], axis=2)
    pe, w1cat, w2 = _compress_weights(cmp_par)
    pe_k, w1_k, _, pe_v, w1_v, _ = cmp_par
    bias = jnp.stack([jnp.einsum('sd,sdh->h', pe_k, w1_k, precision=HI),
                      jnp.einsum('sd,sdh->h', pe_v, w1_v, precision=HI)]).reshape(2, 1, HD)
    part = _paged_compress(cache_cmp.reshape(n_pool, PAGE_SIZE * 2 * NSA_KV, HD), page_table, w1cat)
    w2f = jnp.stack([cmp_par[2], cmp_par[5]])
    o_c, idx = _dec_select(part, bias, w2f, q.reshape(dbs, NSA_HEADS, HD), pos)
    sel_idx = idx[:, :NSA_KV, :SLC_TOP].reshape(-1)
    ng = proj[:, COL_NG:COL_NG + N_GATE].reshape(dbs, NSA_KV, NSA_HG, 3)
    o_nsa = _dec_slc_win(cache_slc.reshape(n_pool, PAGE_SIZE, 4 * HD), page_table, sel_idx,
                         q_rot.reshape(dbs, NSA_KV, NSA_HG, HD), slc_new.reshape(dbs, 2, NSA_KV, HD),
                         cache_win.reshape(dbs, cache_win.shape[1], 4 * HD), win_new.reshape(dbs, 2, NSA_KV, HD),
                         o_c.reshape(dbs, NSA_KV, NSA_HG, HD), ng, last_block=pos // SLC_BLOCK)
    return o_nsa.reshape(dbs, NSA_W), cmp_new, slc_new, win_new


def _rope_partial(x, pos):
    half = ROPE_DIM // 2
    inv = ROPE_THETA ** (-jnp.arange(0, ROPE_DIM, 2, dtype=F32) / ROPE_DIM)
    ang = pos.astype(F32)[:, None] * inv[None, :]
    cos = jnp.cos(ang)[:, None, :]
    sin = jnp.sin(ang)[:, None, :]
    xr = x[..., :ROPE_DIM]
    x1, x2 = xr[..., :half], xr[..., half:]
    rot = jnp.concatenate([x1 * cos - x2 * sin, x2 * cos + x1 * sin], axis=-1)
    return jnp.concatenate([rot, x[..., ROPE_DIM:]], axis=-1)


def _masked_softmax(s, mask):
    s = jnp.where(mask, s, -jnp.inf)
    m = jnp.max(s, axis=-1, keepdims=True)
    m = jnp.where(jnp.isfinite(m), m, 0.0)
    e = jnp.where(mask, jnp.exp(s - m), 0.0)
    d = jnp.sum(e, axis=-1, keepdims=True)
    return e / jnp.where(d > 0, d, 1.0)


def _nsa_compress(rows, pe, w1, w2, prec):
    b, l = rows.shape[:2]
    n_sub = CMP_BLOCK // CMP_STRIDE
    nc = (l - CMP_BLOCK) // CMP_STRIDE + 1
    nch = l // CMP_STRIDE
    ch = rows[:, :nch * CMP_STRIDE].reshape(b, nch, CMP_STRIDE, NSA_KV, HD)
    acc = 0.0
    for r in range(n_sub):
        sl = slice(r * CMP_STRIDE, (r + 1) * CMP_STRIDE)
        pr = jnp.einsum('bnsgd,sdh->bngh', ch + pe[sl][:, None, :], w1[sl], precision=prec)
        acc = acc + pr[:, r:r + nc]
    return jnp.einsum('bngh,hd->bngd', jax.nn.silu(acc), w2, precision=prec)


def _cmp_to_slc(p_cmp, ns):
    a = SLC_BLOCK // CMP_STRIDE
    b = CMP_BLOCK // CMP_STRIDE
    nc = p_cmp.shape[-1]
    total = a * ns + a + b
    pad_cfg = [(0, 0)] * (p_cmp.ndim - 1) + [(b - 1, total - (b - 1) - nc)]
    pp = jnp.pad(p_cmp, pad_cfg)
    out = 0.0
    for m in range(a):
        for n in range(b):
            st = a - 1 - m - n + (b - 1)
            out = out + pp[..., st:st + a * ns:a]
    return out


def _nsa_cmp_slc(q, q_rot, pos, kc, vc, ks_rows, vs_rows, prec):
    b, t = q.shape[:2]
    l = ks_rows.shape[1]
    nc = kc.shape[1]
    ns = -(-l // SLC_BLOCK)
    nsel = min(SLC_TOP, ns)
    cmp_end = jnp.arange(nc) * CMP_STRIDE + CMP_BLOCK - 1
    blk_ids = jnp.arange(ns)
    key_pos = jnp.arange(l)
    key_blk = key_pos // SLC_BLOCK

    def one_block(args):
        qr, qo, ps = args
        s = jnp.einsum('bqghd,bngd->bqghn', qr, kc, precision=prec) * SCALE
        mc = (cmp_end[None, :] <= ps[:, None])[None, :, None, None, :]
        p = _masked_softmax(s, mc)
        o_c = jnp.einsum('bqghn,bngd->bqghd', p, vc, precision=prec)
        imp = _cmp_to_slc(jnp.sum(p, axis=3), ns)
        cur = ps // SLC_BLOCK
        vis = blk_ids[None, :] * SLC_BLOCK <= ps[:, None]
        forced = (blk_ids[None, :] == 0) | (blk_ids[None, :] == cur[:, None]) | (blk_ids[None, :] == cur[:, None] - 1)
        score = jnp.where(forced[None, :, None, :], jnp.inf, imp)
        score = jnp.where(vis[None, :, None, :], score, -jnp.inf)
        _, idx = lax.top_k(score, nsel)
        sel = jnp.any(idx[..., None] == blk_ids, axis=-2)
        ms = sel[..., key_blk] & (key_pos[None, :] <= ps[:, None])[None, :, None, :]
        ss = jnp.einsum('bqghd,bkgd->bqghk', qo, ks_rows, precision=prec) * SCALE
        pss = _masked_softmax(ss, ms[:, :, :, None, :])
        o_s = jnp.einsum('bqghk,bkgd->bqghd', pss, vs_rows, precision=prec)
        return o_c, o_s

    qb = math.gcd(t, QBLOCK)
    nqb = t // qb

    def split_q(a):
        return a.reshape(b, nqb, qb, NSA_KV, NSA_HG, HD).transpose(1, 0, 2, 3, 4, 5)

    def merge_q(a):
        return a.transpose(1, 0, 2, 3, 4, 5).reshape(b, t, NSA_HEADS, HD)

    o_c, o_s = lax.map(one_block, (split_q(q), split_q(q_rot), pos.reshape(nqb, qb)))
    return merge_q(o_c), merge_q(o_s)


def _nsa_window_banded(q_rot, win_rows, pos):
    b, t = q_rot.shape[:2]
    qb = math.gcd(t, QBLOCK)
    nb = t // qb
    span = qb + WINDOW
    rows = jnp.pad(win_rows, ((0, 0), (WINDOW, 0), (0, 0), (0, 0), (0, 0)))
    idx = jnp.arange(nb)[:, None] * qb + jnp.arange(span)[None, :]
    kv = rows[:, idx]
    kpos = idx - WINDOW
    delta = pos.reshape(nb, qb)[:, :, None] - kpos[:, None, :]
    mask = (delta >= 0) & (delta <= WINDOW) & (kpos[:, None, :] >= 0)
    q = q_rot.reshape(b, nb, qb, NSA_KV, NSA_HG, HD)
    s = jnp.einsum('bnqghd,bnkgd->bnqghk', q, kv[:, :, :, 0]) * SCALE
    p = _masked_softmax(s, mask[None, :, :, None, None, :])
    o = jnp.einsum('bnqghk,bnkgd->bnqghd', p, kv[:, :, :, 1])
    return o.reshape(b, t, NSA_HEADS, HD)


def _nsa_window_buffer(q_rot, win_all, qpos, kpos, prec):
    b, t = q_rot.shape[:2]
    q = q_rot.reshape(b, t, NSA_KV, NSA_HG, HD)
    s = jnp.einsum('bqghd,bkgd->bqghk', q, win_all[:, :, 0], precision=prec) * SCALE
    delta = qpos[:, None] - kpos[None, :]
    mask = ((delta >= 0) & (delta <= WINDOW))[None, :, None, None, :]
    p = _masked_softmax(s, mask)
    o = jnp.einsum('bqghk,bkgd->bqghd', p, win_all[:, :, 1], precision=prec)
    return o.reshape(b, t, NSA_HEADS, HD)


def _nsa_combine(q, q_rot, pos, cmp_rows, slc_rows, o_win, gate_logits, cmp_par, prec):
    b, t = q.shape[:2]
    pe_k, w1_k, w2_k, pe_v, w1_v, w2_v = cmp_par
    kc = _nsa_compress(cmp_rows[:, :, 0], pe_k, w1_k, w2_k, None)
    vc = _nsa_compress(cmp_rows[:, :, 1], pe_v, w1_v, w2_v, None)
    o_c, o_s = _nsa_cmp_slc(q, q_rot, pos, kc, vc, slc_rows[:, :, 0], slc_rows[:, :, 1], prec)
    g = jax.nn.sigmoid(gate_logits).reshape(b, t, NSA_HEADS, 3)
    o = g[..., 0:1] * o_c + g[..., 1:2] * o_s + g[..., 2:3] * o_win
    return o.reshape(b, t, NSA_W)


def _hgrn2_chunked(q, k, v, logf, s0, prec):
    b, t, h, dk = q.shape
    dv = v.shape[-1]
    c = math.gcd(t, HG_CHUNK)
    n = t // c

    def time_major(a):
        return a.reshape(b, n, c, h, a.shape[-1]).transpose(1, 0, 3, 2, 4)

    causal = jnp.arange(c)[:, None] >= jnp.arange(c)[None, :]

    def step(S, inp):
        qc, kc, vc, gc = inp
        bb = jnp.cumsum(gc, axis=2)
        inter = jnp.einsum('bhtd,bhde->bhte', qc * jnp.exp(bb), S, precision=prec)
        diff = bb[:, :, :, None, :] - bb[:, :, None, :, :]
        decay = jnp.exp(jnp.where(causal[:, :, None], diff, -jnp.inf))
        A = jnp.einsum('bhtd,bhsd,bhtsd->bhts', qc, kc, decay, precision=prec)
        o = inter + jnp.einsum('bhts,bhse->bhte', A, vc, precision=prec)
        b_last = bb[:, :, -1:, :]
        S = jnp.exp(b_last[:, :, 0, :])[..., None] * S + jnp.einsum(
            'bhsd,bhse->bhde', kc * jnp.exp(b_last - bb), vc, precision=prec)
        return S, o

    S, o = lax.scan(step, s0, (time_major(q), time_major(k), time_major(v), time_major(logf)))
    return o.transpose(1, 0, 3, 2, 4).reshape(b, t, h, dv), S


def _hgrn2_mixer(hq, hf, hv, hg, lb, g_norm, s0, prec):
    b, t = hq.shape[:2]
    q = hq.reshape(b, t, HG_HEADS, HG_DK)
    lbh = lb.reshape(HG_HEADS, HG_DK)
    f = lbh + (1.0 - lbh) * jax.nn.sigmoid(hf.reshape(b, t, HG_HEADS, HG_DK))
    logf = jnp.log(f)
    k = 1.0 - f
    v = hv.reshape(b, t, HG_HEADS, HG_DV)
    o, S = _hgrn2_chunked(q, k, v, logf, s0, prec)
    gn = g_norm.reshape(HG_HEADS, HG_DV)
    o = o * lax.rsqrt(jnp.mean(o * o, axis=-1, keepdims=True) + NORM_EPS) * gn
    o = o.reshape(b, t, HG_VW) * jax.nn.silu(hg)
    return o, S


def _split_proj(p, b, t, pos):
    q = p[:, COL_Q:COL_Q + NSA_W].reshape(b, t, NSA_HEADS, HD)
    q_rot = _rope_partial(q, pos)
    kv6 = p[:, COL_KV:COL_KV + 6 * NSA_KV * HD].reshape(b, t, 6, NSA_KV, HD)
    cmp_rows = kv6[:, :, 0:2]
    slc_rows = jnp.stack([_rope_partial(kv6[:, :, 2], pos), kv6[:, :, 3]], axis=2)
    win_rows = jnp.stack([_rope_partial(kv6[:, :, 4], pos), kv6[:, :, 5]], axis=2)
    ng = p[:, COL_NG:COL_NG + N_GATE].reshape(b, t, N_GATE)
    hq = p[:, COL_HQ:COL_HQ + HG_KW].reshape(b, t, HG_KW)
    hf = p[:, COL_HF:COL_HF + HG_KW].reshape(b, t, HG_KW)
    hv = p[:, COL_HV:COL_HV + HG_VW].reshape(b, t, HG_VW)
    hg = p[:, COL_HGATE:COL_HGATE + HG_VW].reshape(b, t, HG_VW)
    return q, q_rot, cmp_rows, slc_rows, win_rows, ng, (hq, hf, hv, hg)


def _paged_rows(pool, page_table):
    rows = pool[page_table]
    db, n_pages = page_table.shape
    return rows.reshape((db, n_pages * PAGE_SIZE) + rows.shape[3:])


def _reorder_w_in(w):
    o_q, o_kv = 0, NSA_W
    o_ng = o_kv + 6 * NSA_KV * HD
    o_hq = o_ng + N_GATE
    o_ga = o_hq + 2 * HG_KW + 2 * HG_VW
    pad = jnp.zeros((w.shape[0], D_IN_PAD - COL_NG - N_GATE), w.dtype)
    return jnp.concatenate([w[:, o_ga:o_ga + 2 * D_MODEL], w[:, o_q:o_ng], w[:, o_hq:o_ga],
                            w[:, o_ng:o_hq], pad], axis=1)


def _moe_group(top_idx, n_tok):
    n_assign = n_tok * TOP_K
    n_blk = -(-n_assign // MOE_ROWS) + N_EXPERTS
    n_rows = n_blk * MOE_ROWS
    e_flat = top_idx.reshape(-1)
    onehot = (e_flat[:, None] == jnp.arange(N_EXPERTS, dtype=jnp.int32)[None, :]).astype(jnp.int32)
    csum = jnp.cumsum(onehot, axis=0)
    rank = jnp.sum((csum - onehot) * onehot, axis=1)
    counts = csum[-1]
    padded = (counts + MOE_ROWS - 1) // MOE_ROWS * MOE_ROWS
    pend = jnp.cumsum(padded)
    pstart = pend - padded
    dest = pstart[e_flat] + rank
    tok_flat = jnp.arange(n_assign, dtype=jnp.int32) // TOP_K
    row_tok = jnp.full((n_rows,), n_tok, jnp.int32).at[dest].set(tok_flat)
    blk_e = jnp.minimum(jnp.sum(jnp.arange(n_blk)[:, None] * MOE_ROWS >= pend[None, :], axis=1), N_EXPERTS - 1)
    nused = (pend[-1] // MOE_ROWS).reshape(1)
    return row_tok, dest.reshape(n_tok, TOP_K), blk_e.astype(jnp.int32), nused.astype(jnp.int32)


def kernel(x_prompt, x_sample, cache_cmp_kv, cache_slc_kv, cache_win_kv, state_hgrn, cache_mem_kv, page_table, mem_prompt, norm_mix, w_in, cmp_pe_k, cmp_w1_k, cmp_w2_k, cmp_pe_v, cmp_w1_v, cmp_w2_v, lb_logits, hg_norm, w_branch_a, w_branch_b, w_out, norm_x, wq_x, wk_x, wv_x, wo_x, norm_ffn, w_router, b_router, w_gu, b_gu, w_dn, b_dn, norm_final):
    bsz, seq, d = x_prompt.shape
    dbs, dec_seq, _ = x_sample.shape
    past = page_table.shape[1] * PAGE_SIZE
    win_buf_len = cache_win_kv.shape[2]
    n_p = bsz * seq
    n_s = dbs * dec_seq
    pos_p = jnp.arange(seq, dtype=jnp.int32)
    pos_s = past + jnp.arange(dec_seq, dtype=jnp.int32)
    kpos_w = jnp.arange(past - win_buf_len, past + dec_seq, dtype=jnp.int32)
    lb_all = jnp.cumsum(jax.nn.softmax(lb_logits, axis=0), axis=0)
    l = 0
    cmp_par = (cmp_pe_k[l], cmp_w1_k[l], cmp_w2_k[l], cmp_pe_v[l], cmp_w1_v[l], cmp_w2_v[l])
    w_in_r = _reorder_w_in(w_in[l])
    wa, wb, wo = w_branch_a[l], w_branch_b[l], w_out[l]
    xp = x_prompt.reshape(n_p, d)
    xs = x_sample.reshape(n_s, d)

    proj_p = _norm_matmul(xp, norm_mix[l], w_in_r.astype(BF16), normalize=True, hp=False, tm=1024, tn=PROJ_TN)
    o_nsa, cmp_rows, slc_rows, win_rows = _nsa_prompt(proj_p, bsz, seq, pos_p, cmp_par)
    o_hg, s_final = _hgrn2_prompt(proj_p, bsz, seq, lb_all[l], hg_norm[l])
    xp1 = _merge(xp, o_nsa.reshape(n_p, NSA_W), o_hg, proj_p,
                 wa.astype(BF16), wb.astype(BF16), wo.astype(BF16), hp=False, tm=256)

    wkv = jnp.concatenate([wk_x[l], wv_x[l]], axis=1).astype(BF16)
    mem2d = mem_prompt.reshape(bsz * mem_prompt.shape[1], d)
    mkv = _norm_matmul(mem2d, norm_x[l], wkv, normalize=False, hp=False, tm=mem2d.shape[0], tn=PROJ_TN)
    mkv = mkv.reshape(bsz, mem_prompt.shape[1], 2 * X_W)
    xp2 = _cross(xp1.reshape(bsz, seq, d), norm_x[l], mkv, wq_x[l].astype(BF16), wo_x[l].astype(BF16), tm=512)
    xp2 = xp2.reshape(n_p, d)

    proj_s = _norm_matmul(xs, norm_mix[l], w_in_r, normalize=True, hp=True, tm=n_s, tn=PROJ_TN)
    assert dec_seq == 1
    o_nsa_s, cmp_new, slc_new, win_new = _nsa_decode(proj_s, past, cache_cmp_kv[l], cache_slc_kv[l],
                                                      cache_win_kv[l], page_table, cmp_par)
    win_keep = jnp.concatenate([cache_win_kv[l], win_new], axis=1)[:, dec_seq:]
    o_hg_s, s_new = _hgrn2_step(proj_s, state_hgrn[l], lb_all[l], hg_norm[l])
    xs1 = _merge(xs, o_nsa_s, o_hg_s, proj_s, wa, wb, wo, hp=True, tm=n_s)

    hs = _rms(xs1, norm_x[l]).reshape(dbs, dec_seq, d)
    mem_s = cache_mem_kv[l]
    qx = jnp.einsum('btd,dw->btw', hs, wq_x[l], precision=HI).reshape(dbs, dec_seq, X_HEADS, X_HD)
    sx = jnp.einsum('bthd,bmhd->bhtm', qx, mem_s[:, :, 0], precision=HI) * X_SCALE
    px = jax.nn.softmax(sx, axis=-1)
    ox = jnp.einsum('bhtm,bmhd->bthd', px, mem_s[:, :, 1], precision=HI).reshape(n_s, X_W)
    xs2 = xs1 + jnp.dot(ox, wo_x[l], precision=HI)

    n_tok = n_p + n_s
    w_r_pad = jnp.pad(w_router[l], ((0, 0), (0, LANES - N_EXPERTS)))
    b_r_pad = jnp.pad(b_router[l], (0, LANES - N_EXPERTS)).reshape(1, LANES)
    h_p, logit_p = _router(xp2, norm_ffn[l], w_r_pad, b_r_pad, tm=512)
    h_s, logit_s = _router(xs2, norm_ffn[l], w_r_pad, b_r_pad, tm=n_s)
    logits = jnp.concatenate([logit_p, logit_s], axis=0)[:, :N_EXPERTS]
    top_val, top_idx = lax.top_k(logits, TOP_K)
    gates = jax.nn.softmax(top_val, axis=-1)
    row_tok, pos, blk_e, nused = _moe_group(top_idx, n_tok)
    h_all = jnp.concatenate([h_p, h_s, jnp.zeros((1, d), BF16)], axis=0)
    xb = h_all[row_tok]
    yb = _moe_experts(xb, blk_e, nused, w_gu[l], b_gu[l], w_dn[l], b_dn[l])
    yg = yb[pos.T]
    y_prompt = _combine(xp2, yg, gates, norm_final, tm=256, row0=0)
    y_sample = _combine(xs2, yg, gates, norm_final, tm=n_s, row0=n_p)

    mkv_out = mkv.reshape(bsz, mem_prompt.shape[1], 2, X_HEADS, X_HD)
    return (y_prompt.reshape(bsz, seq, d), y_sample.reshape(dbs, dec_seq, d),
            cmp_rows[None], slc_rows[None], win_rows[:, seq - min(WINDOW, seq):][None], s_final[None],
            mkv_out[None], cmp_new[None], slc_new[None],
            win_keep[None], s_new[None])
```

```python
import functools
import math

import jax
import jax.numpy as jnp
from jax import lax
from jax.experimental import pallas as pl
from jax.experimental.pallas import tpu as pltpu

F32 = jnp.float32
BF16 = jnp.bfloat16
HI = lax.Precision.HIGHEST

D_MODEL = 2048
PAGE_SIZE = 128
NSA_HEADS = 8
NSA_KV = 2
NSA_HG = NSA_HEADS // NSA_KV
HD = 128
CMP_BLOCK = 32
CMP_STRIDE = 16
SLC_BLOCK = 64
SLC_TOP = 16
WINDOW = 512
QBLOCK = 128
ROPE_DIM = HD // 4
ROPE_THETA = 500000.0
SCALE = HD ** -0.5
QK_SCALE = SCALE * math.log2(math.e)
HG_HEADS = 8
HG_DK = 128
HG_DV = 128
HG_CHUNK = 64
X_HEADS = 4
X_HD = 128
X_W = X_HEADS * X_HD
X_SCALE = X_HD ** -0.5
N_EXPERTS = 32
TOP_K = 4
D_FF = D_MODEL
SWIGLU_LIMIT = 7.0
SWIGLU_ALPHA = 1.702
NORM_EPS = 1e-5
NSA_W = NSA_HEADS * HD
HG_KW = HG_HEADS * HG_DK
HG_VW = HG_HEADS * HG_DV

LANES = 128
MXU_N = 256
VMEM_LIMIT = 52 * 1024 * 1024

COL_GA = 0
COL_GB = COL_GA + D_MODEL
COL_Q = COL_GB + D_MODEL
COL_KV = COL_Q + NSA_W
COL_HQ = COL_KV + 6 * NSA_KV * HD
COL_HF = COL_HQ + HG_KW
COL_HV = COL_HF + HG_KW
COL_HGATE = COL_HV + HG_VW
COL_NG = COL_HGATE + HG_VW
N_GATE = NSA_HEADS * 3
PROJ_TN = 512
D_IN_PAD = -(-(COL_NG + N_GATE) // PROJ_TN) * PROJ_TN

MOE_ROWS = 256
MOE_TN = 1024


def _params(sem):
    return pltpu.CompilerParams(dimension_semantics=sem, vmem_limit_bytes=VMEM_LIMIT)


def _rms(x, g):
    return x * lax.rsqrt(jnp.mean(x * x, axis=-1, keepdims=True) + NORM_EPS) * g


def _dot(a, b, hp):
    if hp:
        return jnp.dot(a.astype(F32), b.astype(F32), precision=HI, preferred_element_type=F32)
    return jnp.dot(a.astype(BF16), b.astype(BF16), preferred_element_type=F32)


def _norm_matmul_kernel(x_ref, g_ref, w_ref, o_ref, h_ref, *, normalize, hp):
    @pl.when(pl.program_id(1) == 0)
    def _():
        x = x_ref[...]
        if normalize:
            x = _rms(x, g_ref[...])
        h_ref[...] = x.astype(h_ref.dtype)

    o_ref[...] = _dot(h_ref[...], w_ref[...], hp)


def _norm_matmul(x, g, w, *, normalize, hp, tm, tn):
    m, k = x.shape
    n = w.shape[1]
    return pl.pallas_call(
        functools.partial(_norm_matmul_kernel, normalize=normalize, hp=hp),
        out_shape=jax.ShapeDtypeStruct((m, n), F32),
        grid=(m // tm, n // tn),
        in_specs=[pl.BlockSpec((tm, k), lambda i, j: (i, 0)),
                  pl.BlockSpec((1, k), lambda i, j: (0, 0)),
                  pl.BlockSpec((k, tn), lambda i, j: (0, j))],
        out_specs=pl.BlockSpec((tm, tn), lambda i, j: (i, j)),
        scratch_shapes=[pltpu.VMEM((tm, k), F32 if hp else BF16)],
        compiler_params=_params(("parallel", "arbitrary")),
    )(x, g.reshape(1, k), w)


def _merge_kernel(x_ref, on_ref, oh_ref, ga_ref, gb_ref, wa_ref, wb_ref, wo_ref, o_ref, *, hp):
    a = _dot(on_ref[...], wa_ref[...], hp)
    b = _dot(oh_ref[...], wb_ref[...], hp)
    m = jax.nn.sigmoid(ga_ref[...]) * a + jax.nn.sigmoid(gb_ref[...]) * b
    o_ref[...] = x_ref[...] + _dot(m, wo_ref[...], hp)


def _merge(x, o_nsa, o_hg, proj, wa, wb, wo, *, hp, tm):
    m = x.shape[0]
    d = D_MODEL
    const = lambda i: (0, 0)
    return pl.pallas_call(
        functools.partial(_merge_kernel, hp=hp),
        out_shape=jax.ShapeDtypeStruct((m, d), F32),
        grid=(m // tm,),
        in_specs=[pl.BlockSpec((tm, d), lambda i: (i, 0)),
                  pl.BlockSpec((tm, NSA_W), lambda i: (i, 0)),
                  pl.BlockSpec((tm, HG_VW), lambda i: (i, 0)),
                  pl.BlockSpec((tm, d), lambda i: (i, COL_GA // d)),
                  pl.BlockSpec((tm, d), lambda i: (i, COL_GB // d)),
                  pl.BlockSpec((NSA_W, d), const, pipeline_mode=pl.Buffered(1)),
                  pl.BlockSpec((HG_VW, d), const, pipeline_mode=pl.Buffered(1)),
                  pl.BlockSpec((d, d), const, pipeline_mode=pl.Buffered(1))],
        out_specs=pl.BlockSpec((tm, d), lambda i: (i, 0)),
        compiler_params=_params(("parallel",)),
    )(x, o_nsa, o_hg, proj, proj, wa, wb, wo)


def _cross_kernel(x_ref, g_ref, k_ref, v_ref, wq_ref, wo_ref, o_ref):
    x = x_ref[0]
    h = _rms(x, g_ref[...])
    q = _dot(h, wq_ref[...], False)
    k = k_ref[0].astype(BF16)
    v = v_ref[0].astype(BF16)
    outs = []
    for hd in range(X_HEADS):
        sl = slice(hd * X_HD, (hd + 1) * X_HD)
        s = lax.dot_general(q[:, sl].astype(BF16), k[:, sl], (((1,), (1,)), ((), ())),
                            preferred_element_type=F32) * X_SCALE
        s = s - jnp.max(s, axis=-1, keepdims=True)
        e = jnp.exp(s)
        p = e / jnp.sum(e, axis=-1, keepdims=True)
        outs.append(jnp.dot(p.astype(BF16), v[:, sl], preferred_element_type=F32))
    o = jnp.concatenate(outs, axis=-1)
    o_ref[0] = x + _dot(o, wo_ref[...], False)


def _cross(x, g, mkv, wq, wo, *, tm):
    b, t, d = x.shape
    mlen = mkv.shape[1]
    const = lambda bi, i: (0, 0)
    return pl.pallas_call(
        _cross_kernel,
        out_shape=jax.ShapeDtypeStruct((b, t, d), F32),
        grid=(b, t // tm),
        in_specs=[pl.BlockSpec((1, tm, d), lambda bi, i: (bi, i, 0)),
                  pl.BlockSpec((1, d), const),
                  pl.BlockSpec((1, mlen, X_W), lambda bi, i: (bi, 0, 0)),
                  pl.BlockSpec((1, mlen, X_W), lambda bi, i: (bi, 0, 1)),
                  pl.BlockSpec((d, X_W), const),
                  pl.BlockSpec((X_W, d), const)],
        out_specs=pl.BlockSpec((1, tm, d), lambda bi, i: (bi, i, 0)),
        compiler_params=_params(("parallel", "parallel")),
    )(x, g.reshape(1, d), mkv, mkv, wq, wo)


def _router_kernel(x_ref, g_ref, w_ref, b_ref, h_ref, v_ref, i_ref):
    h = _rms(x_ref[...], g_ref[...])
    h_ref[...] = h.astype(BF16)
    logits = _dot(h, w_ref[...], True) + b_ref[...]
    lane = lax.broadcasted_iota(jnp.int32, logits.shape, 1)
    score = jnp.where(lane < N_EXPERTS, logits, -jnp.inf)
    vals = jnp.zeros(logits.shape, F32)
    ids = jnp.zeros(logits.shape, jnp.int32)
    for r in range(TOP_K):
        best = jnp.max(score, axis=1, keepdims=True)
        idx = jnp.min(jnp.where(score == best, lane, LANES), axis=1, keepdims=True)
        vals = jnp.where(lane == r, best, vals)
        ids = jnp.where(lane == r, idx, ids)
        score = jnp.where(lane == idx, -jnp.inf, score)
    v_ref[...] = vals
    i_ref[...] = ids


def _router(x, g, w_r_pad, b_r_pad, *, tm):
    m, d = x.shape
    const = lambda i: (0, 0)
    row = lambda i: (i, 0)
    return pl.pallas_call(
        _router_kernel,
        out_shape=(jax.ShapeDtypeStruct((m, d), BF16), jax.ShapeDtypeStruct((m, LANES), F32),
                   jax.ShapeDtypeStruct((m, LANES), jnp.int32)),
        grid=(m // tm,),
        in_specs=[pl.BlockSpec((tm, d), row),
                  pl.BlockSpec((1, d), const),
                  pl.BlockSpec((d, LANES), const),
                  pl.BlockSpec((1, LANES), const)],
        out_specs=(pl.BlockSpec((tm, d), row), pl.BlockSpec((tm, LANES), row), pl.BlockSpec((tm, LANES), row)),
        compiler_params=_params(("parallel",)),
    )(x, g.reshape(1, d), w_r_pad, b_r_pad)


def _new_weight_tile(blk_e_ref, i):
    prev = blk_e_ref[jnp.maximum(i - 1, 0)]
    return (i == 0) | (blk_e_ref[i] != prev)


def _moe_gu_kernel(blk_e_ref, nused_ref, x_ref, w_ref, b_ref, sel_ref, o_ref, wbf_ref):
    i = pl.program_id(1)
    tn = w_ref.shape[2]

    @pl.when(_new_weight_tile(blk_e_ref, i))
    def _():
        wbf_ref[...] = w_ref[0].astype(BF16)

    @pl.when(i < nused_ref[0])
    def _():
        gu = jnp.dot(x_ref[...], wbf_ref[...], preferred_element_type=F32) + b_ref[0]
        gate = jnp.minimum(gu, SWIGLU_LIMIT)
        up = jnp.clip(gu, -SWIGLU_LIMIT, SWIGLU_LIMIT)
        up_next = pltpu.roll(up, tn - 1, axis=1)
        act = (up_next + 1.0) * gate * jax.nn.sigmoid(gate * SWIGLU_ALPHA)
        lane = lax.broadcasted_iota(jnp.int32, act.shape, 1)
        act = jnp.where(lane % 2 == 0, act, 0.0).astype(BF16)
        for c in range(tn // MXU_N):
            o_ref[:, c * LANES:(c + 1) * LANES] = jnp.dot(
                act[:, c * MXU_N:(c + 1) * MXU_N], sel_ref[...], preferred_element_type=F32).astype(BF16)

    @pl.when(i >= nused_ref[0])
    def _():
        o_ref[...] = jnp.zeros_like(o_ref)


def _moe_dn_kernel(blk_e_ref, nused_ref, a_ref, w_ref, b_ref, o_ref, wbf_ref):
    i = pl.program_id(1)

    @pl.when(_new_weight_tile(blk_e_ref, i))
    def _():
        wbf_ref[...] = w_ref[0].astype(BF16)

    @pl.when(i < nused_ref[0])
    def _():
        y = jnp.dot(a_ref[...], wbf_ref[...], preferred_element_type=F32) + b_ref[0]
        o_ref[...] = y.astype(o_ref.dtype)

    @pl.when(i >= nused_ref[0])
    def _():
        o_ref[...] = jnp.zeros_like(o_ref)


def _moe_experts(xb, blk_e, nused, w_gu, b_gu, w_dn, b_dn):
    n_rows, d = xb.shape
    n_blk = n_rows // MOE_ROWS
    tn = MOE_TN
    sel = (jnp.arange(MXU_N)[:, None] == 2 * jnp.arange(LANES)[None, :]).astype(BF16)
    act = pl.pallas_call(
        _moe_gu_kernel,
        out_shape=jax.ShapeDtypeStruct((n_rows, D_FF), BF16),
        grid_spec=pltpu.PrefetchScalarGridSpec(
            num_scalar_prefetch=2,
            grid=(2 * D_FF // tn, n_blk),
            in_specs=[pl.BlockSpec((MOE_ROWS, d), lambda j, i, be, nu: (i, 0)),
                      pl.BlockSpec((1, d, tn), lambda j, i, be, nu: (be[i], 0, j)),
                      pl.BlockSpec((1, 1, tn), lambda j, i, be, nu: (be[i], 0, j)),
                      pl.BlockSpec((MXU_N, LANES), lambda j, i, be, nu: (0, 0))],
            out_specs=pl.BlockSpec((MOE_ROWS, tn // 2), lambda j, i, be, nu: (i, j)),
            scratch_shapes=[pltpu.VMEM((d, tn), BF16)]),
        compiler_params=_params(("arbitrary", "arbitrary")),
    )(blk_e, nused, xb, w_gu, b_gu.reshape(N_EXPERTS, 1, 2 * D_FF), sel)
    return pl.pallas_call(
        _moe_dn_kernel,
        out_shape=jax.ShapeDtypeStruct((n_rows, d), BF16),
        grid_spec=pltpu.PrefetchScalarGridSpec(
            num_scalar_prefetch=2,
            grid=(d // tn, n_blk),
            in_specs=[pl.BlockSpec((MOE_ROWS, D_FF), lambda j, i, be, nu: (i, 0)),
                      pl.BlockSpec((1, D_FF, tn), lambda j, i, be, nu: (be[i], 0, j)),
                      pl.BlockSpec((1, 1, tn), lambda j, i, be, nu: (be[i], 0, j))],
            out_specs=pl.BlockSpec((MOE_ROWS, tn), lambda j, i, be, nu: (i, j)),
            scratch_shapes=[pltpu.VMEM((D_FF, tn), BF16)]),
        compiler_params=_params(("arbitrary", "arbitrary")),
    )(blk_e, nused, act, w_dn, b_dn.reshape(N_EXPERTS, 1, d))


def _combine_kernel(x_ref, y_ref, gt_ref, g_ref, o_ref):
    acc = x_ref[...]
    gt = gt_ref[...]
    for k in range(TOP_K):
        acc = acc + y_ref[k].astype(F32) * gt[:, k:k + 1]
    o_ref[...] = _rms(acc, g_ref[...])


def _combine(x, yg, gates, g, *, tm, row0):
    m, d = x.shape
    off = row0 // tm
    return pl.pallas_call(
        _combine_kernel,
        out_shape=jax.ShapeDtypeStruct((m, d), F32),
        grid=(m // tm,),
        in_specs=[pl.BlockSpec((tm, d), lambda i: (i, 0)),
                  pl.BlockSpec((TOP_K, tm, d), lambda i: (0, i + off, 0)),
                  pl.BlockSpec((tm, TOP_K), lambda i: (i + off, 0)),
                  pl.BlockSpec((1, d), lambda i: (0, 0))],
        out_specs=pl.BlockSpec((tm, d), lambda i: (i, 0)),
        compiler_params=_params(("parallel",)),
    )(x, yg, gates, g.reshape(1, d))


def _rope_tile(x, cos_t, sin_lo, sin_hi):
    half = ROPE_DIM // 2
    return x * cos_t + pltpu.roll(x, half, axis=1) * sin_hi + pltpu.roll(x, HD - half, axis=1) * sin_lo


def _nsa_prep_kernel(q_ref, cmp_ref, slc_ref, win_ref, cos_ref, slo_ref, shi_ref,
                     cmp_o, slc_o, win_o, qtc_o, qtr_o, ks_o, vst_o, kw_o, vwt_o):
    cos_t, sin_lo, sin_hi = cos_ref[...], slo_ref[...], shi_ref[...]
    cmp_o[...] = cmp_ref[...]
    for g in range(NSA_KV):
        for h in range(NSA_HG):
            col = (g * NSA_HG + h) * HD
            qh = q_ref[:, col:col + HD]
            q_rot = _rope_tile(qh, cos_t, sin_lo, sin_hi)
            qtc_o[0, g, 0, :, h * HD:(h + 1) * HD] = (qh * QK_SCALE).T.astype(BF16)
            qtr_o[0, g, 0, :, h * HD:(h + 1) * HD] = (q_rot * QK_SCALE).T.astype(BF16)
        kcol = slice(g * HD, (g + 1) * HD)
        vcol = slice((NSA_KV + g) * HD, (NSA_KV + g + 1) * HD)
        for src, rows_o, k_o, vt_o in ((slc_ref, slc_o, ks_o, vst_o), (win_ref, win_o, kw_o, vwt_o)):
            k_rot = _rope_tile(src[:, kcol], cos_t, sin_lo, sin_hi)
            v = src[:, vcol]
            rows_o[:, kcol] = k_rot
            rows_o[:, vcol] = v
            k_o[0, g, 0] = k_rot.astype(BF16)
            vt_o[0, g, 0] = v.T.astype(BF16)


def _rope_tables(pos):
    half = ROPE_DIM // 2
    inv = ROPE_THETA ** (-jnp.arange(0, ROPE_DIM, 2, dtype=F32) / ROPE_DIM)
    ang = pos.astype(F32)[:, None] * inv[None, :]
    cos, sin = jnp.cos(ang), jnp.sin(ang)
    t = pos.shape[0]
    ones = jnp.ones((t, HD - ROPE_DIM), F32)
    zeros = jnp.zeros((t, HD - ROPE_DIM), F32)
    zh = jnp.zeros((t, half), F32)
    cos_t = jnp.concatenate([cos, cos, ones], axis=1)
    sin_lo = jnp.concatenate([-sin, zh, zeros], axis=1)
    sin_hi = jnp.concatenate([zh, sin, zeros], axis=1)
    return cos_t, sin_lo, sin_hi


def _nsa_prep(proj, bsz, seq, pos):
    n = bsz * seq
    tq = QBLOCK
    nq = seq // tq
    kvw = 2 * NSA_KV * HD
    cos_t, sin_lo, sin_hi = _rope_tables(pos)
    row = lambda r: (r, 0)
    tab = lambda r: (r % nq, 0)
    blk5 = lambda r: (r // nq, 0, r % nq, 0, 0)
    assert nq % SLC_TILES == 0
    big_k = lambda r: (r // nq, 0, (r % nq) // SLC_TILES, (r % nq) % SLC_TILES, 0)
    big_vt = lambda r: (r // nq, 0, (r % nq) // SLC_TILES, 0, (r % nq) % SLC_TILES)
    qt_shape = jax.ShapeDtypeStruct((bsz, NSA_KV, nq, HD, NSA_HG * tq), BF16)
    kv_shape = jax.ShapeDtypeStruct((bsz, NSA_KV, nq, tq, HD), BF16)
    ks_shape = jax.ShapeDtypeStruct((bsz, NSA_KV, nq // SLC_TILES, SLC_TILES * tq, HD), BF16)
    vst_shape = jax.ShapeDtypeStruct((bsz, NSA_KV, nq // SLC_TILES, HD, SLC_TILES * tq), BF16)
    rows_shape = jax.ShapeDtypeStruct((n, kvw), F32)
    return pl.pallas_call(
        _nsa_prep_kernel,
        out_shape=(rows_shape, rows_shape, rows_shape, qt_shape, qt_shape, ks_shape, vst_shape, kv_shape, kv_shape),
        grid=(n // tq,),
        in_specs=[pl.BlockSpec((tq, NSA_W), lambda r: (r, COL_Q // NSA_W)),
                  pl.BlockSpec((tq, kvw), lambda r: (r, COL_KV // kvw)),
                  pl.BlockSpec((tq, kvw), lambda r: (r, COL_KV // kvw + 1)),
                  pl.BlockSpec((tq, kvw), lambda r: (r, COL_KV // kvw + 2)),
                  pl.BlockSpec((tq, HD), tab), pl.BlockSpec((tq, HD), tab), pl.BlockSpec((tq, HD), tab)],
        out_specs=(pl.BlockSpec((tq, kvw), row), pl.BlockSpec((tq, kvw), row), pl.BlockSpec((tq, kvw), row),
                   pl.BlockSpec((1, NSA_KV, 1, HD, NSA_HG * tq), blk5),
                   pl.BlockSpec((1, NSA_KV, 1, HD, NSA_HG * tq), blk5),
                   pl.BlockSpec((1, NSA_KV, 1, tq, HD), big_k), pl.BlockSpec((1, NSA_KV, 1, HD, tq), big_vt),
                   pl.BlockSpec((1, NSA_KV, 1, tq, HD), blk5), pl.BlockSpec((1, NSA_KV, 1, HD, tq), blk5)),
        compiler_params=_params(("parallel",)),
    )(proj, proj, proj, proj, cos_t, sin_lo, sin_hi)


def _compress_kernel(x_ref, pe_ref, w1_ref, w2_ref, nat_o, tr_o):
    nch = x_ref.shape[1] // CMP_STRIDE
    acc = jnp.zeros((nch, 2 * HD), F32)
    bias = [jnp.zeros((8, HD), F32) for _ in range(CMP_BLOCK // CMP_STRIDE)]
    for s in range(CMP_STRIDE):
        w = w1_ref[0, s]
        rows = x_ref[0, pl.ds(s, nch, stride=CMP_STRIDE), :]
        acc = acc + jnp.dot(rows.astype(BF16), w, preferred_element_type=F32)
        for r in range(CMP_BLOCK // CMP_STRIDE):
            pe_row = jnp.broadcast_to(pe_ref[0, r * CMP_STRIDE + s:r * CMP_STRIDE + s + 1, :], (8, HD))
            pb = jnp.dot(pe_row.astype(BF16), w, preferred_element_type=F32)
            bias[r] = bias[r] + pb[:, r * HD:(r + 1) * HD]
    first = acc[:, :HD] + bias[0][0:1]
    second = acc[:, HD:] + bias[1][0:1]
    pre = first + pltpu.roll(second, nch - 1, axis=0)
    out = jnp.dot(jax.nn.silu(pre).astype(BF16), w2_ref[0], preferred_element_type=F32)
    row = lax.broadcasted_iota(jnp.int32, out.shape, 0)
    out = jnp.where(row < nch - 1, out, 0.0)
    nat_o[0, 0] = out.astype(BF16)
    tr_o[0, 0] = out.T.astype(BF16)


def _compress_weights(cmp_par):
    pe_k, w1_k, w2_k, pe_v, w1_v, w2_v = cmp_par

    def cat(w1):
        return jnp.concatenate([w1[:CMP_STRIDE], w1[CMP_STRIDE:]], axis=2)

    pe = jnp.stack([pe_k, pe_v])
    w1 = jnp.stack([cat(w1_k), cat(w1_v)]).astype(BF16)
    w2 = jnp.stack([w2_k, w2_v]).astype(BF16)
    return pe, w1, w2


def _nsa_compress_rows(rows, cmp_par):
    bsz, length, _ = rows.shape
    nch = length // CMP_STRIDE
    ncol = 2 * NSA_KV
    pe, w1, w2 = _compress_weights(cmp_par)
    return pl.pallas_call(
        _compress_kernel,
        out_shape=(jax.ShapeDtypeStruct((bsz, ncol, nch, HD), BF16), jax.ShapeDtypeStruct((bsz, ncol, HD, nch), BF16)),
        grid=(bsz, ncol),
        in_specs=[pl.BlockSpec((1, length, HD), lambda b, c: (b, 0, c)),
                  pl.BlockSpec((1, CMP_BLOCK, HD), lambda b, c: (c // NSA_KV, 0, 0)),
                  pl.BlockSpec((1, CMP_STRIDE, HD, 2 * HD), lambda b, c: (c // NSA_KV, 0, 0, 0)),
                  pl.BlockSpec((1, HD, HD), lambda b, c: (c // NSA_KV, 0, 0))],
        out_specs=(pl.BlockSpec((1, 1, nch, HD), lambda b, c: (b, c, 0, 0)),
                   pl.BlockSpec((1, 1, HD, nch), lambda b, c: (b, c, 0, 0))),
        compiler_params=_params(("parallel", "parallel")),
    )(rows, pe, w1, w2)


def _imp_matrix_t(nch, ns):
    a = SLC_BLOCK // CMP_STRIDE
    b = CMP_BLOCK // CMP_STRIDE
    j = jnp.arange(ns)[:, None]
    n = jnp.arange(nch)[None, :]
    out = jnp.zeros((ns, nch), F32)
    for m in range(a):
        for k in range(b):
            out = out + (n == a * j + a - 1 - m - k).astype(F32)
    return out


SLC_TILES = 4


def _flash_t(k_ref, vt_ref, qt, acc_ref, hi, bias_fn):
    width = qt.shape[1]
    acc_ref[...] = jnp.zeros_like(acc_ref)

    def body(it, carry):
        m_old, l_old = carry
        s = jnp.dot(k_ref[0, 0, it], qt, preferred_element_type=F32) + bias_fn(it)
        m_new = jnp.maximum(m_old, jnp.max(s, axis=0, keepdims=True))
        m_safe = jnp.where(m_new == -jnp.inf, 0.0, m_new)
        alpha = jnp.exp2(m_old - m_safe)
        p = jnp.exp2(s - m_safe)
        l_new = alpha * l_old + jnp.sum(p, axis=0, keepdims=True)
        acc_ref[...] = alpha * acc_ref[...] + jnp.dot(vt_ref[0, 0, it], p.astype(BF16), preferred_element_type=F32)
        return m_new, l_new

    init = (jnp.full((1, width), -jnp.inf, F32), jnp.zeros((1, width), F32))
    _, l_fin = lax.fori_loop(0, hi, body, init)
    return acc_ref[...] / jnp.where(l_fin > 0, l_fin, 1.0)


def _nsa_attn_kernel(qtc_ref, qtr_ref, kc_ref, vct_ref, imp_ref, ks_ref, vst_ref, kw_ref, vwt_ref, ng_ref,
                     o_ref, acc_ref, sel_ref):
    qi = pl.program_id(2)
    tq = QBLOCK
    nch = kc_ref.shape[2]
    ns = imp_ref.shape[0]
    tile4 = lambda a: jnp.concatenate([a] * NSA_HG, axis=1)
    pos = qi * tq + lax.broadcasted_iota(jnp.int32, (1, tq), 1)

    s = jnp.dot(kc_ref[0, 0], qtc_ref[0, 0, 0], preferred_element_type=F32)
    cmp_end = lax.broadcasted_iota(jnp.int32, (nch, tq), 0) * CMP_STRIDE + (CMP_BLOCK - 1)
    mask_c = tile4(cmp_end <= pos)
    s = jnp.where(mask_c, s, -jnp.inf)
    m = jnp.max(s, axis=0, keepdims=True)
    m = jnp.where(m == -jnp.inf, 0.0, m)
    e = jnp.exp2(s - m)
    d = jnp.sum(e, axis=0, keepdims=True)
    p = e / jnp.where(d > 0, d, 1.0)
    o_c = jnp.dot(vct_ref[0, 0], p.astype(BF16), preferred_element_type=F32)
    p_grp = p[:, 0:tq]
    for h in range(1, NSA_HG):
        p_grp = p_grp + p[:, h * tq:(h + 1) * tq]
    imp = jnp.dot(imp_ref[...], p_grp, precision=HI, preferred_element_type=F32)

    blk = lax.broadcasted_iota(jnp.int32, (ns, tq), 0)
    cur = pos // SLC_BLOCK
    forced = (blk == 0) | (blk == cur) | (blk == cur - 1)
    score = jnp.where(forced, jnp.inf, imp)
    score = jnp.where(blk <= cur, score, -jnp.inf)
    sel = jnp.zeros((ns, tq), F32)
    for _ in range(min(SLC_TOP, ns)):
        best = jnp.max(score, axis=0, keepdims=True)
        idx = jnp.min(jnp.where(score == best, blk, ns), axis=0, keepdims=True)
        pick = blk == idx
        sel = jnp.where(pick, 1.0, sel)
        score = jnp.where(pick, -jnp.inf, score)
    sel_ref[...] = sel

    big = SLC_TILES * tq
    per_step = big // SLC_BLOCK
    key_in_step = lax.broadcasted_iota(jnp.int32, (big, tq), 0)

    def bias_slc(it):
        chosen = jnp.concatenate([jnp.broadcast_to(sel_ref[pl.ds(it * per_step + b, 1), :], (SLC_BLOCK, tq))
                                  for b in range(per_step)], axis=0)
        ok = (chosen > 0.5) & (it * big + key_in_step <= pos)
        return tile4(jnp.where(ok, 0.0, -jnp.inf))

    qtr = qtr_ref[0, 0, 0]
    o_s = _flash_t(ks_ref, vst_ref, qtr, acc_ref, qi // SLC_TILES + 1, bias_slc)

    key_in_tile = lax.broadcasted_iota(jnp.int32, (tq, tq), 0)
    win_tiles = [qi - WINDOW // tq + j for j in range(WINDOW // tq + 1)]
    biases = []
    for kt in win_tiles:
        key_pos = kt * tq + key_in_tile
        delta = pos - key_pos
        biases.append(jnp.where((delta >= 0) & (delta <= WINDOW) & (key_pos >= 0), 0.0, -jnp.inf))
    k_win = jnp.concatenate([kw_ref[0, 0, jnp.maximum(kt, 0)] for kt in win_tiles], axis=0)
    s = jnp.dot(k_win, qtr, preferred_element_type=F32) + tile4(jnp.concatenate(biases, axis=0))
    e = jnp.exp2(s - jnp.max(s, axis=0, keepdims=True))
    d = jnp.sum(e, axis=0, keepdims=True)
    o_w = jnp.zeros_like(o_c)
    for j, kt in enumerate(win_tiles):
        o_w = o_w + jnp.dot(vwt_ref[0, 0, jnp.maximum(kt, 0)], e[j * tq:(j + 1) * tq].astype(BF16),
                            preferred_element_type=F32)
    o_w = o_w / d

    gates = jax.nn.sigmoid(ng_ref[0, 0])
    for h in range(NSA_HG):
        gate = gates[3 * h:3 * h + 3]
        sl = slice(h * tq, (h + 1) * tq)
        o = gate[0:1] * o_c[:, sl] + gate[1:2] * o_s[:, sl] + gate[2:3] * o_w[:, sl]
        o_ref[0, :, h * HD:(h + 1) * HD] = o.T.astype(o_ref.dtype)


def _nsa_attn(qtc, qtr, kc_nat, kc_tr, ks5, vst5, kw5, vwt5, ng_t):
    bsz, _, nq, _, width = qtc.shape
    tq = QBLOCK
    seq = nq * tq
    nch = kc_nat.shape[2]
    ns = seq // SLC_BLOCK
    imp_t = _imp_matrix_t(nch, ns)
    q_spec = pl.BlockSpec((1, 1, 1, HD, width), lambda b, g, i: (b, g, i, 0, 0))
    k5_spec = pl.BlockSpec((1, 1, nq, tq, HD), lambda b, g, i: (b, g, 0, 0, 0))
    whole = lambda a: pl.BlockSpec((1, 1) + a.shape[2:], lambda b, g, i: (b, g, 0, 0, 0))
    return pl.pallas_call(
        _nsa_attn_kernel,
        out_shape=jax.ShapeDtypeStruct((bsz, seq, NSA_W), BF16),
        grid=(bsz, NSA_KV, nq),
        in_specs=[q_spec, q_spec,
                  pl.BlockSpec((1, 1, nch, HD), lambda b, g, i: (b, g, 0, 0)),
                  pl.BlockSpec((1, 1, HD, nch), lambda b, g, i: (b, NSA_KV + g, 0, 0)),
                  pl.BlockSpec((ns, nch), lambda b, g, i: (0, 0)),
                  whole(ks5), whole(vst5), k5_spec, k5_spec,
                  pl.BlockSpec((1, 1, N_GATE // NSA_KV, tq), lambda b, g, i: (b, g, 0, i))],
        out_specs=pl.BlockSpec((1, tq, NSA_HG * HD), lambda b, g, i: (b, i, g)),
        scratch_shapes=[pltpu.VMEM((HD, width), F32), pltpu.VMEM((ns, tq), F32)],
        compiler_params=_params(("parallel", "parallel", "arbitrary")),
    )(qtc, qtr, kc_nat, kc_tr, imp_t, ks5, vst5, kw5, vwt5, ng_t)


def _nsa_prompt(proj, bsz, seq, pos, cmp_par):
    cmp_rows, slc_rows, win_rows, qtc, qtr, ks5, vst5, kw5, vwt5 = _nsa_prep(proj, bsz, seq, pos)
    kc_nat, kc_tr = _nsa_compress_rows(cmp_rows.reshape(bsz, seq, -1), cmp_par)
    ng_t = proj[:, COL_NG:COL_NG + N_GATE].reshape(bsz, seq, NSA_KV, N_GATE // NSA_KV).transpose(0, 2, 3, 1)
    o_nsa = _nsa_attn(qtc, qtr, kc_nat, kc_tr, ks5, vst5, kw5, vwt5, ng_t)
    shape5 = (bsz, seq, 2, NSA_KV, HD)
    return o_nsa, cmp_rows.reshape(shape5), slc_rows.reshape(shape5), win_rows.reshape(shape5)


HG_TILE = 128
HG_DIAG = 8


def _split3(x):
    hi = x.astype(BF16)
    r1 = x - hi.astype(F32)
    mid = r1.astype(BF16)
    lo = (r1 - mid.astype(F32)).astype(BF16)
    return hi, mid, lo


def _hgrn2_kernel(hq_ref, hf_ref, hv_ref, hg_ref, lb_ref, gn_ref, o_ref, s_ref, st_ref):
    i = pl.program_id(2)
    c = HG_TILE

    @pl.when(i == 0)
    def _():
        st_ref[...] = jnp.zeros_like(st_ref)

    lb = lb_ref[0]
    q = hq_ref[...]
    f = lb + (1.0 - lb) * jax.nn.sigmoid(hf_ref[...])
    logf = jnp.log(f)
    k = 1.0 - f
    v = hv_ref[...]
    row = lax.broadcasted_iota(jnp.int32, (c, c), 0)
    col = lax.broadcasted_iota(jnp.int32, (c, c), 1)
    tri = (row >= col).astype(BF16)
    b = sum(jnp.dot(tri, part, preferred_element_type=F32) for part in _split3(logf))
    b_last = b[c - 1:c, :]

    o = (jnp.sum(q * k, axis=1, keepdims=True)) * v
    for delta in range(1, HG_DIAG):
        valid = (row % HG_DIAG) >= delta
        decay = jnp.exp(jnp.where(valid, b - pltpu.roll(b, delta, axis=0), -jnp.inf))
        a = jnp.sum(q * pltpu.roll(k, delta, axis=0) * decay, axis=1, keepdims=True)
        o = o + a * pltpu.roll(v, delta, axis=0)

    a_mat = jnp.zeros((c, c), F32)
    blk = 2 * HG_DIAG
    while blk <= c:
        half = blk // 2
        b_mid = jnp.concatenate(
            [jnp.broadcast_to(b[s + half - 1:s + half, :], (blk, b.shape[1])) for s in range(0, c, blk)], axis=0)
        second = (row % blk) >= half
        qs = q * jnp.exp(jnp.where(second, b - b_mid, -jnp.inf))
        ks = k * jnp.exp(jnp.where(second, -jnp.inf, b_mid - b))
        a_blk = lax.dot_general(qs.astype(BF16), ks.astype(BF16), (((1,), (1,)), ((), ())),
                                preferred_element_type=F32)
        a_mat = a_mat + jnp.where((row // blk) == (col // blk), a_blk, 0.0)
        blk *= 2

    st = st_ref[...]
    o = o + jnp.dot(a_mat.astype(BF16), v.astype(BF16), preferred_element_type=F32)
    o = o + lax.dot_general((q * jnp.exp(b)).astype(BF16), st.astype(BF16), (((1,), (1,)), ((), ())),
                            preferred_element_type=F32)
    k_end = (k * jnp.exp(b_last - b)).astype(BF16)
    st_new = st * jnp.exp(b_last) + jnp.dot(v.T.astype(BF16), k_end, preferred_element_type=F32)
    st_ref[...] = st_new

    o = o * lax.rsqrt(jnp.mean(o * o, axis=1, keepdims=True) + NORM_EPS) * gn_ref[0]
    o_ref[...] = (o * jax.nn.silu(hg_ref[...])).astype(o_ref.dtype)

    @pl.when(i == pl.num_programs(2) - 1)
    def _():
        s_ref[0, 0] = st_new.T


def _hgrn2_prompt(proj, bsz, seq, lb, g_norm):
    c = HG_TILE
    nc = seq // c
    col = lambda base: (lambda b, h, i: (b * nc + i, base // HG_DK + h))
    vec = pl.BlockSpec((1, 1, HG_DK), lambda b, h, i: (h, 0, 0))
    return pl.pallas_call(
        _hgrn2_kernel,
        out_shape=(jax.ShapeDtypeStruct((bsz * seq, HG_VW), BF16),
                   jax.ShapeDtypeStruct((bsz, HG_HEADS, HG_DK, HG_DV), F32)),
        grid=(bsz, HG_HEADS, nc),
        in_specs=[pl.BlockSpec((c, HG_DK), col(COL_HQ)), pl.BlockSpec((c, HG_DK), col(COL_HF)),
                  pl.BlockSpec((c, HG_DV), col(COL_HV)), pl.BlockSpec((c, HG_DV), col(COL_HGATE)), vec, vec],
        out_specs=(pl.BlockSpec((c, HG_DV), lambda b, h, i: (b * nc + i, h)),
                   pl.BlockSpec((1, 1, HG_DK, HG_DV), lambda b, h, i: (b, h, 0, 0))),
        scratch_shapes=[pltpu.VMEM((HG_DV, HG_DK), F32)],
        compiler_params=_params(("parallel", "parallel", "arbitrary")),
    )(proj, proj, proj, proj, lb.reshape(HG_HEADS, 1, HG_DK), g_norm.reshape(HG_HEADS, 1, HG_DV))


PAGES_PER_STEP = 16


def _paged_compress_kernel(pt_ref, *refs):
    pages = refs[:PAGES_PER_STEP]
    wh_ref, wl_ref = refs[PAGES_PER_STEP], refs[PAGES_PER_STEP + 1]
    o_ref = refs[PAGES_PER_STEP + 2]
    per_page = PAGE_SIZE // CMP_STRIDE
    ncol = 2 * NSA_KV
    for c in range(ncol):
        acc = jnp.zeros((PAGES_PER_STEP * per_page, 2 * HD), F32)
        for s2 in range(CMP_STRIDE // 2):
            def rows(s):
                return jnp.concatenate(
                    [p[0, pl.ds(s * ncol + c, per_page, stride=CMP_STRIDE * ncol), :] for p in pages], axis=0)
            lhs = jnp.concatenate([rows(2 * s2), rows(2 * s2 + 1)], axis=1)
            hi = lhs.astype(BF16)
            lo = (lhs - hi.astype(F32)).astype(BF16)
            wh = wh_ref[c // NSA_KV, s2]
            acc = acc + jnp.dot(hi, wh, preferred_element_type=F32) + jnp.dot(lo, wh, preferred_element_type=F32)
            acc = acc + jnp.dot(hi, wl_ref[c // NSA_KV, s2], preferred_element_type=F32)
        o_ref[0, c] = acc


def _paged_compress(cache, page_table, w1cat):
    dbs, n_pages = page_table.shape
    per_page = PAGE_SIZE // CMP_STRIDE
    w_st = w1cat.reshape(2, CMP_STRIDE // 2, 2 * HD, 2 * HD)
    w_hi = w_st.astype(BF16)
    w_lo = (w_st - w_hi.astype(F32)).astype(BF16)
    w_spec = pl.BlockSpec((2, CMP_STRIDE // 2, 2 * HD, 2 * HD), lambda b, s, pt: (0, 0, 0, 0))
    page_spec = lambda k: pl.BlockSpec((1, PAGE_SIZE * 2 * NSA_KV, HD),
                                       lambda b, s, pt: (pt[b, s * PAGES_PER_STEP + k], 0, 0))
    return pl.pallas_call(
        _paged_compress_kernel,
        out_shape=jax.ShapeDtypeStruct((dbs, 2 * NSA_KV, n_pages * per_page, 2 * HD), F32),
        grid_spec=pltpu.PrefetchScalarGridSpec(
            num_scalar_prefetch=1,
            grid=(dbs, n_pages // PAGES_PER_STEP),
            in_specs=[page_spec(k) for k in range(PAGES_PER_STEP)] + [w_spec, w_spec],
            out_specs=pl.BlockSpec((1, 2 * NSA_KV, PAGES_PER_STEP * per_page, 2 * HD), lambda b, s, pt: (b, 0, s, 0))),
        compiler_params=_params(("parallel", "arbitrary")),
    )(page_table, *([cache] * PAGES_PER_STEP), w_hi, w_lo)


def _dec_select_kernel(p_ref, bias_ref, w2_ref, q_ref, imp_ref, oc_ref, idx_ref, *, pos):
    nch = p_ref.shape[2]
    ns_pad = imp_ref.shape[1]
    ns = pos // SLC_BLOCK + 1
    blocks = []
    for c in range(2 * NSA_KV):
        part = p_ref[0, c]
        pre = part[:, :HD] + pltpu.roll(part[:, HD:], nch - 1, axis=0) + bias_ref[c // NSA_KV]
        out = jnp.dot(jax.nn.silu(pre), w2_ref[c // NSA_KV], precision=HI, preferred_element_type=F32)
        blocks.append(out)
    q = q_ref[0]
    n_col = lax.broadcasted_iota(jnp.int32, (nch, 1), 0)
    valid = (n_col * CMP_STRIDE + (CMP_BLOCK - 1) <= pos) & (n_col < nch - 1)
    lane = lax.broadcasted_iota(jnp.int32, (1, ns_pad), 1)
    out_lane = lax.broadcasted_iota(jnp.int32, (NSA_HEADS, LANES), 1)
    out_row = lax.broadcasted_iota(jnp.int32, (NSA_HEADS, LANES), 0)
    head_row = lax.broadcasted_iota(jnp.int32, (NSA_HEADS, HD), 0)
    cur = pos // SLC_BLOCK
    o_c = jnp.zeros((NSA_HEADS, HD), F32)
    idx_out = jnp.zeros((NSA_HEADS, LANES), jnp.int32)
    for g in range(NSA_KV):
        kc = blocks[g]
        vc = jnp.where(valid, blocks[NSA_KV + g], 0.0)
        p_grp = jnp.zeros((nch, 1), F32)
        for h in range(NSA_HG):
            hh = g * NSA_HG + h
            s = jnp.sum(kc * q[hh:hh + 1, :], axis=1, keepdims=True) * SCALE
            s = jnp.where(valid, s, -jnp.inf)
            m = jnp.max(s, axis=0, keepdims=True)
            m = jnp.where(m == -jnp.inf, 0.0, m)
            e = jnp.exp(s - m)
            d = jnp.sum(e, axis=0, keepdims=True)
            p = e / jnp.where(d > 0, d, 1.0)
            o_c = jnp.where(head_row == hh, jnp.sum(p * vc, axis=0, keepdims=True), o_c)
            p_grp = p_grp + p
        imp = jnp.sum(p_grp * imp_ref[...], axis=0, keepdims=True)
        forced = (lane == 0) | (lane == cur) | (lane == cur - 1)
        score = jnp.where(forced, jnp.inf, imp)
        score = jnp.where(lane < ns, score, -jnp.inf)
        for r in range(min(SLC_TOP, ns)):
            best = jnp.max(score, axis=1, keepdims=True)
            idx = jnp.min(jnp.where(score == best, lane, ns_pad), axis=1, keepdims=True)
            score = jnp.where(lane == idx, -jnp.inf, score)
            idx_out = jnp.where((out_lane == r) & (out_row == g), idx, idx_out)
    oc_ref[0] = o_c
    idx_ref[0] = idx_out


def _dec_select(part, bias, w2, q, pos):
    dbs, ncol, nch, _ = part.shape
    ns = pos // SLC_BLOCK + 1
    ns_pad = -(-ns // LANES) * LANES
    imp = jnp.pad(_imp_matrix_t(nch, ns).T, ((0, 0), (0, ns_pad - ns)))
    return pl.pallas_call(
        functools.partial(_dec_select_kernel, pos=pos),
        out_shape=(jax.ShapeDtypeStruct((dbs, NSA_HEADS, HD), F32),
                   jax.ShapeDtypeStruct((dbs, NSA_HEADS, LANES), jnp.int32)),
        grid=(dbs,),
        in_specs=[pl.BlockSpec((1, ncol, nch, 2 * HD), lambda b: (b, 0, 0, 0)),
                  pl.BlockSpec((2, 1, HD), lambda b: (0, 0, 0)),
                  pl.BlockSpec((2, HD, HD), lambda b: (0, 0, 0)),
                  pl.BlockSpec((1, NSA_HEADS, HD), lambda b: (b, 0, 0)),
                  pl.BlockSpec((nch, ns_pad), lambda b: (0, 0))],
        out_specs=(pl.BlockSpec((1, NSA_HEADS, HD), lambda b: (b, 0, 0)),
                   pl.BlockSpec((1, NSA_HEADS, LANES), lambda b: (b, 0, 0))),
        compiler_params=_params(("parallel",)),
    )(part, bias, w2, q, imp)


def _dec_attend(q, key_tiles, val_tiles, valids, k_new, v_new):
    head_row = lax.broadcasted_iota(jnp.int32, q.shape, 0)
    out = jnp.zeros(q.shape, F32)
    for h in range(q.shape[0]):
        qh = q[h:h + 1, :]
        scores = []
        for kt, ok in zip(key_tiles, valids):
            s = jnp.sum(kt * qh, axis=1, keepdims=True) * SCALE
            scores.append(s if ok is None else jnp.where(ok, s, -jnp.inf))
        s_new = jnp.sum(qh * k_new, axis=1, keepdims=True) * SCALE
        m = s_new
        for s in scores:
            m = jnp.maximum(m, jnp.max(s, axis=0, keepdims=True))
        e_new = jnp.exp(s_new - m)
        den = e_new
        acc = e_new * v_new
        for s, vt in zip(scores, val_tiles):
            e = jnp.exp(s - m)
            den = den + jnp.sum(e, axis=0, keepdims=True)
            acc = acc + jnp.sum(e * vt, axis=0, keepdims=True)
        out = jnp.where(head_row == h, acc / den, out)
    return out


def _dec_slc_win_kernel(pt_ref, idx_ref, *refs, last_block):
    nsel = SLC_TOP
    blks = refs[:nsel]
    q_ref, snew_ref, win_ref, wnew_ref, oc_ref, ng_ref, o_ref = refs[nsel:]
    b, g = pl.program_id(0), pl.program_id(1)
    ncol = 2 * NSA_KV

    def column(r, c0):
        out = r[0, pl.ds(c0, r.shape[1] // ncol, stride=ncol), :]
        for gg in range(1, NSA_KV):
            out = jnp.where(g == gg, r[0, pl.ds(c0 + gg, r.shape[1] // ncol, stride=ncol), :], out)
        return out

    keys = lambda r: column(r, 0)
    vals = lambda r: column(r, NSA_KV)
    q = q_ref[0, 0]
    base = (b * NSA_KV + g) * nsel
    valids = [idx_ref[base + k] != last_block for k in range(nsel)]
    o_s = _dec_attend(q, [keys(r) for r in blks], [vals(r) for r in blks], valids,
                      snew_ref[0, 0, pl.ds(g, 1), :], snew_ref[0, 1, pl.ds(g, 1), :])
    o_w = _dec_attend(q, [keys(win_ref)], [vals(win_ref)], [None],
                      wnew_ref[0, 0, pl.ds(g, 1), :], wnew_ref[0, 1, pl.ds(g, 1), :])
    gate = jax.nn.sigmoid(ng_ref[0, 0])
    o_ref[0, 0] = gate[:, 0:1] * oc_ref[0, 0] + gate[:, 1:2] * o_s + gate[:, 2:3] * o_w


def _dec_slc_win(slc_cache, page_table, sel_idx, q_rot, slc_new, win_cache, win_new, o_c, ng, *, last_block):
    dbs = q_rot.shape[0]
    per_page = PAGE_SIZE // SLC_BLOCK
    ncol = 2 * NSA_KV
    halves = slc_cache.reshape(slc_cache.shape[0] * per_page, SLC_BLOCK * ncol, HD)
    wrows = win_cache.shape[1]

    def blk_spec(k):
        def index(b, g, pt, idx):
            j = jnp.minimum(idx[(b * NSA_KV + g) * SLC_TOP + k], last_block - 1)
            return (pt[b * (page_table.shape[1]) + j // per_page] * per_page + j % per_page, 0, 0)
        return pl.BlockSpec((1, SLC_BLOCK * ncol, HD), index)

    grp4 = pl.BlockSpec((1, 1, NSA_HG, HD), lambda b, g, pt, idx: (b, g, 0, 0))
    new4 = pl.BlockSpec((1, 2, NSA_KV, HD), lambda b, g, pt, idx: (b, 0, 0, 0))
    return pl.pallas_call(
        functools.partial(_dec_slc_win_kernel, last_block=last_block),
        out_shape=jax.ShapeDtypeStruct((dbs, NSA_KV, NSA_HG, HD), F32),
        grid_spec=pltpu.PrefetchScalarGridSpec(
            num_scalar_prefetch=2,
            grid=(dbs, NSA_KV),
            in_specs=[blk_spec(k) for k in range(SLC_TOP)]
            + [grp4, new4,
               pl.BlockSpec((1, wrows, HD), lambda b, g, pt, idx: (b, 0, 0)),
               new4, grp4,
               pl.BlockSpec((1, 1, NSA_HG, 3), lambda b, g, pt, idx: (b, g, 0, 0))],
            out_specs=grp4),
        compiler_params=_params(("parallel", "parallel")),
    )(page_table.reshape(-1), sel_idx, *([halves] * SLC_TOP), q_rot, slc_new, win_cache, win_new, o_c, ng)


def _hgrn2_step_kernel(hq_ref, hf_ref, hv_ref, hg_ref, lb_ref, gn_ref, s_ref, o_ref, so_ref):
    dbs = hq_ref.shape[0]
    lb = lb_ref[0]
    f_all = lb + (1.0 - lb) * jax.nn.sigmoid(hf_ref[...])
    row = lax.broadcasted_iota(jnp.int32, (HG_DK, HG_DK), 0)
    col = lax.broadcasted_iota(jnp.int32, (HG_DK, HG_DK), 1)
    diag = row == col

    def as_col(v_row):
        return jnp.sum(jnp.where(diag, v_row, 0.0), axis=1, keepdims=True)

    for b in range(dbs):
        f = f_all[b:b + 1]
        s_new = as_col(f) * s_ref[b, 0] + as_col(1.0 - f) * hv_ref[b:b + 1, :]
        so_ref[b, 0] = s_new
        o_ref[b:b + 1, :] = jnp.sum(s_new * as_col(hq_ref[b:b + 1, :]), axis=0, keepdims=True)
    o = o_ref[...]
    o = o * lax.rsqrt(jnp.mean(o * o, axis=1, keepdims=True) + NORM_EPS) * gn_ref[0]
    o_ref[...] = o * jax.nn.silu(hg_ref[...])


def _hgrn2_step(proj, state, lb, g_norm):
    dbs = proj.shape[0]
    col = lambda base: (lambda h: (0, base // HG_DK + h))
    vec = pl.BlockSpec((1, 1, HG_DK), lambda h: (h, 0, 0))
    st = pl.BlockSpec((dbs, 1, HG_DK, HG_DV), lambda h: (0, h, 0, 0))
    return pl.pallas_call(
        _hgrn2_step_kernel,
        out_shape=(jax.ShapeDtypeStruct((dbs, HG_VW), F32), jax.ShapeDtypeStruct(state.shape, F32)),
        grid=(HG_HEADS,),
        in_specs=[pl.BlockSpec((dbs, HG_DK), col(COL_HQ)), pl.BlockSpec((dbs, HG_DK), col(COL_HF)),
                  pl.BlockSpec((dbs, HG_DV), col(COL_HV)), pl.BlockSpec((dbs, HG_DV), col(COL_HGATE)),
                  vec, vec, st],
        out_specs=(pl.BlockSpec((dbs, HG_DV), lambda h: (0, h)), st),
        compiler_params=_params(("parallel",)),
    )(proj, proj, proj, proj, lb.reshape(HG_HEADS, 1, HG_DK), g_norm.reshape(HG_HEADS, 1, HG_DV), state)


def _nsa_decode(proj, pos, cache_cmp, cache_slc, cache_win, page_table, cmp_par):
    dbs = proj.shape[0]
    n_pool = cache_cmp.shape[0]
    assert pos % SLC_BLOCK == 0 and pos % CMP_STRIDE == 0 and pos == page_table.shape[1] * PAGE_SIZE
    assert cache_win.shape[1] <= WINDOW
    pos_arr = jnp.full((1,), pos, jnp.int32)
    q = proj[:, COL_Q:COL_Q + NSA_W].reshape(dbs, 1, NSA_HEADS, HD)
    q_rot = _rope_partial(q, pos_arr)
    kv6 = proj[:, COL_KV:COL_KV + 6 * NSA_KV * HD].reshape(dbs, 1, 6, NSA_KV, HD)
    cmp_new = kv6[:, :, 0:2]
    slc_new = jnp.stack([_rope_partial(kv6[:, :, 2], pos_arr), kv6[:, :, 3]], axis=2)
    win_new = jnp.stack([_rope_partial(kv6[:, :, 4], pos_arr), kv6[:, :, 5]], axis=2)
    pe_k, w1_k, _, pe_v, w1_v, _ = cmp_par
    w1cat = jnp.stack([jnp.concatenate([w[:CMP_STRIDE], w[CMP_STRIDE:]], axis=2) for w in (w1_k, w1_v)])
    bias = jnp.stack([jnp.einsum('sd,sdh->h', pe_k, w1_k, precision=HI),
                      jnp.einsum('sd,sdh->h', pe_v, w1_v, precision=HI)]).reshape(2, 1, HD)
    part = _paged_compress(cache_cmp.reshape(n_pool, PAGE_SIZE * 2 * NSA_KV, HD), page_table, w1cat)
    w2f = jnp.stack([cmp_par[2], cmp_par[5]])
    o_c, idx = _dec_select(part, bias, w2f, q.reshape(dbs, NSA_HEADS, HD), pos)
    sel_idx = idx[:, :NSA_KV, :SLC_TOP].reshape(-1)
    ng = proj[:, COL_NG:COL_NG + N_GATE].reshape(dbs, NSA_KV, NSA_HG, 3)
    o_nsa = _dec_slc_win(cache_slc.reshape(n_pool, PAGE_SIZE * 2 * NSA_KV, HD), page_table, sel_idx,
                         q_rot.reshape(dbs, NSA_KV, NSA_HG, HD), slc_new.reshape(dbs, 2, NSA_KV, HD),
                         cache_win.reshape(dbs, cache_win.shape[1] * 2 * NSA_KV, HD),
                         win_new.reshape(dbs, 2, NSA_KV, HD),
                         o_c.reshape(dbs, NSA_KV, NSA_HG, HD), ng, last_block=pos // SLC_BLOCK)
    return o_nsa.reshape(dbs, NSA_W), cmp_new, slc_new, win_new


def _rope_partial(x, pos):
    half = ROPE_DIM // 2
    inv = ROPE_THETA ** (-jnp.arange(0, ROPE_DIM, 2, dtype=F32) / ROPE_DIM)
    ang = pos.astype(F32)[:, None] * inv[None, :]
    cos = jnp.cos(ang)[:, None, :]
    sin = jnp.sin(ang)[:, None, :]
    xr = x[..., :ROPE_DIM]
    x1, x2 = xr[..., :half], xr[..., half:]
    rot = jnp.concatenate([x1 * cos - x2 * sin, x2 * cos + x1 * sin], axis=-1)
    return jnp.concatenate([rot, x[..., ROPE_DIM:]], axis=-1)


def _masked_softmax(s, mask):
    s = jnp.where(mask, s, -jnp.inf)
    m = jnp.max(s, axis=-1, keepdims=True)
    m = jnp.where(jnp.isfinite(m), m, 0.0)
    e = jnp.where(mask, jnp.exp(s - m), 0.0)
    d = jnp.sum(e, axis=-1, keepdims=True)
    return e / jnp.where(d > 0, d, 1.0)


def _nsa_compress(rows, pe, w1, w2, prec):
    b, l = rows.shape[:2]
    n_sub = CMP_BLOCK // CMP_STRIDE
    nc = (l - CMP_BLOCK) // CMP_STRIDE + 1
    nch = l // CMP_STRIDE
    ch = rows[:, :nch * CMP_STRIDE].reshape(b, nch, CMP_STRIDE, NSA_KV, HD)
    acc = 0.0
    for r in range(n_sub):
        sl = slice(r * CMP_STRIDE, (r + 1) * CMP_STRIDE)
        pr = jnp.einsum('bnsgd,sdh->bngh', ch + pe[sl][:, None, :], w1[sl], precision=prec)
        acc = acc + pr[:, r:r + nc]
    return jnp.einsum('bngh,hd->bngd', jax.nn.silu(acc), w2, precision=prec)


def _cmp_to_slc(p_cmp, ns):
    a = SLC_BLOCK // CMP_STRIDE
    b = CMP_BLOCK // CMP_STRIDE
    nc = p_cmp.shape[-1]
    total = a * ns + a + b
    pad_cfg = [(0, 0)] * (p_cmp.ndim - 1) + [(b - 1, total - (b - 1) - nc)]
    pp = jnp.pad(p_cmp, pad_cfg)
    out = 0.0
    for m in range(a):
        for n in range(b):
            st = a - 1 - m - n + (b - 1)
            out = out + pp[..., st:st + a * ns:a]
    return out


def _nsa_cmp_slc(q, q_rot, pos, kc, vc, ks_rows, vs_rows, prec):
    b, t = q.shape[:2]
    l = ks_rows.shape[1]
    nc = kc.shape[1]
    ns = -(-l // SLC_BLOCK)
    nsel = min(SLC_TOP, ns)
    cmp_end = jnp.arange(nc) * CMP_STRIDE + CMP_BLOCK - 1
    blk_ids = jnp.arange(ns)
    key_pos = jnp.arange(l)
    key_blk = key_pos // SLC_BLOCK

    def one_block(args):
        qr, qo, ps = args
        s = jnp.einsum('bqghd,bngd->bqghn', qr, kc, precision=prec) * SCALE
        mc = (cmp_end[None, :] <= ps[:, None])[None, :, None, None, :]
        p = _masked_softmax(s, mc)
        o_c = jnp.einsum('bqghn,bngd->bqghd', p, vc, precision=prec)
        imp = _cmp_to_slc(jnp.sum(p, axis=3), ns)
        cur = ps // SLC_BLOCK
        vis = blk_ids[None, :] * SLC_BLOCK <= ps[:, None]
        forced = (blk_ids[None, :] == 0) | (blk_ids[None, :] == cur[:, None]) | (blk_ids[None, :] == cur[:, None] - 1)
        score = jnp.where(forced[None, :, None, :], jnp.inf, imp)
        score = jnp.where(vis[None, :, None, :], score, -jnp.inf)
        _, idx = lax.top_k(score, nsel)
        sel = jnp.any(idx[..., None] == blk_ids, axis=-2)
        ms = sel[..., key_blk] & (key_pos[None, :] <= ps[:, None])[None, :, None, :]
        ss = jnp.einsum('bqghd,bkgd->bqghk', qo, ks_rows, precision=prec) * SCALE
        pss = _masked_softmax(ss, ms[:, :, :, None, :])
        o_s = jnp.einsum('bqghk,bkgd->bqghd', pss, vs_rows, precision=prec)
        return o_c, o_s

    qb = math.gcd(t, QBLOCK)
    nqb = t // qb

    def split_q(a):
        return a.reshape(b, nqb, qb, NSA_KV, NSA_HG, HD).transpose(1, 0, 2, 3, 4, 5)

    def merge_q(a):
        return a.transpose(1, 0, 2, 3, 4, 5).reshape(b, t, NSA_HEADS, HD)

    o_c, o_s = lax.map(one_block, (split_q(q), split_q(q_rot), pos.reshape(nqb, qb)))
    return merge_q(o_c), merge_q(o_s)


def _nsa_window_banded(q_rot, win_rows, pos):
    b, t = q_rot.shape[:2]
    qb = math.gcd(t, QBLOCK)
    nb = t // qb
    span = qb + WINDOW
    rows = jnp.pad(win_rows, ((0, 0), (WINDOW, 0), (0, 0), (0, 0), (0, 0)))
    idx = jnp.arange(nb)[:, None] * qb + jnp.arange(span)[None, :]
    kv = rows[:, idx]
    kpos = idx - WINDOW
    delta = pos.reshape(nb, qb)[:, :, None] - kpos[:, None, :]
    mask = (delta >= 0) & (delta <= WINDOW) & (kpos[:, None, :] >= 0)
    q = q_rot.reshape(b, nb, qb, NSA_KV, NSA_HG, HD)
    s = jnp.einsum('bnqghd,bnkgd->bnqghk', q, kv[:, :, :, 0]) * SCALE
    p = _masked_softmax(s, mask[None, :, :, None, None, :])
    o = jnp.einsum('bnqghk,bnkgd->bnqghd', p, kv[:, :, :, 1])
    return o.reshape(b, t, NSA_HEADS, HD)


def _nsa_window_buffer(q_rot, win_all, qpos, kpos, prec):
    b, t = q_rot.shape[:2]
    q = q_rot.reshape(b, t, NSA_KV, NSA_HG, HD)
    s = jnp.einsum('bqghd,bkgd->bqghk', q, win_all[:, :, 0], precision=prec) * SCALE
    delta = qpos[:, None] - kpos[None, :]
    mask = ((delta >= 0) & (delta <= WINDOW))[None, :, None, None, :]
    p = _masked_softmax(s, mask)
    o = jnp.einsum('bqghk,bkgd->bqghd', p, win_all[:, :, 1], precision=prec)
    return o.reshape(b, t, NSA_HEADS, HD)


def _nsa_combine(q, q_rot, pos, cmp_rows, slc_rows, o_win, gate_logits, cmp_par, prec):
    b, t = q.shape[:2]
    pe_k, w1_k, w2_k, pe_v, w1_v, w2_v = cmp_par
    kc = _nsa_compress(cmp_rows[:, :, 0], pe_k, w1_k, w2_k, None)
    vc = _nsa_compress(cmp_rows[:, :, 1], pe_v, w1_v, w2_v, None)
    o_c, o_s = _nsa_cmp_slc(q, q_rot, pos, kc, vc, slc_rows[:, :, 0], slc_rows[:, :, 1], prec)
    g = jax.nn.sigmoid(gate_logits).reshape(b, t, NSA_HEADS, 3)
    o = g[..., 0:1] * o_c + g[..., 1:2] * o_s + g[..., 2:3] * o_win
    return o.reshape(b, t, NSA_W)


def _hgrn2_chunked(q, k, v, logf, s0, prec):
    b, t, h, dk = q.shape
    dv = v.shape[-1]
    c = math.gcd(t, HG_CHUNK)
    n = t // c

    def time_major(a):
        return a.reshape(b, n, c, h, a.shape[-1]).transpose(1, 0, 3, 2, 4)

    causal = jnp.arange(c)[:, None] >= jnp.arange(c)[None, :]

    def step(S, inp):
        qc, kc, vc, gc = inp
        bb = jnp.cumsum(gc, axis=2)
        inter = jnp.einsum('bhtd,bhde->bhte', qc * jnp.exp(bb), S, precision=prec)
        diff = bb[:, :, :, None, :] - bb[:, :, None, :, :]
        decay = jnp.exp(jnp.where(causal[:, :, None], diff, -jnp.inf))
        A = jnp.einsum('bhtd,bhsd,bhtsd->bhts', qc, kc, decay, precision=prec)
        o = inter + jnp.einsum('bhts,bhse->bhte', A, vc, precision=prec)
        b_last = bb[:, :, -1:, :]
        S = jnp.exp(b_last[:, :, 0, :])[..., None] * S + jnp.einsum(
            'bhsd,bhse->bhde', kc * jnp.exp(b_last - bb), vc, precision=prec)
        return S, o

    S, o = lax.scan(step, s0, (time_major(q), time_major(k), time_major(v), time_major(logf)))
    return o.transpose(1, 0, 3, 2, 4).reshape(b, t, h, dv), S


def _hgrn2_mixer(hq, hf, hv, hg, lb, g_norm, s0, prec):
    b, t = hq.shape[:2]
    q = hq.reshape(b, t, HG_HEADS, HG_DK)
    lbh = lb.reshape(HG_HEADS, HG_DK)
    f = lbh + (1.0 - lbh) * jax.nn.sigmoid(hf.reshape(b, t, HG_HEADS, HG_DK))
    logf = jnp.log(f)
    k = 1.0 - f
    v = hv.reshape(b, t, HG_HEADS, HG_DV)
    o, S = _hgrn2_chunked(q, k, v, logf, s0, prec)
    gn = g_norm.reshape(HG_HEADS, HG_DV)
    o = o * lax.rsqrt(jnp.mean(o * o, axis=-1, keepdims=True) + NORM_EPS) * gn
    o = o.reshape(b, t, HG_VW) * jax.nn.silu(hg)
    return o, S


def _split_proj(p, b, t, pos):
    q = p[:, COL_Q:COL_Q + NSA_W].reshape(b, t, NSA_HEADS, HD)
    q_rot = _rope_partial(q, pos)
    kv6 = p[:, COL_KV:COL_KV + 6 * NSA_KV * HD].reshape(b, t, 6, NSA_KV, HD)
    cmp_rows = kv6[:, :, 0:2]
    slc_rows = jnp.stack([_rope_partial(kv6[:, :, 2], pos), kv6[:, :, 3]], axis=2)
    win_rows = jnp.stack([_rope_partial(kv6[:, :, 4], pos), kv6[:, :, 5]], axis=2)
    ng = p[:, COL_NG:COL_NG + N_GATE].reshape(b, t, N_GATE)
    hq = p[:, COL_HQ:COL_HQ + HG_KW].reshape(b, t, HG_KW)
    hf = p[:, COL_HF:COL_HF + HG_KW].reshape(b, t, HG_KW)
    hv = p[:, COL_HV:COL_HV + HG_VW].reshape(b, t, HG_VW)
    hg = p[:, COL_HGATE:COL_HGATE + HG_VW].reshape(b, t, HG_VW)
    return q, q_rot, cmp_rows, slc_rows, win_rows, ng, (hq, hf, hv, hg)


def _paged_rows(pool, page_table):
    rows = pool[page_table]
    db, n_pages = page_table.shape
    return rows.reshape((db, n_pages * PAGE_SIZE) + rows.shape[3:])


def _reorder_w_in(w):
    o_q, o_kv = 0, NSA_W
    o_ng = o_kv + 6 * NSA_KV * HD
    o_hq = o_ng + N_GATE
    o_ga = o_hq + 2 * HG_KW + 2 * HG_VW
    pad = jnp.zeros((w.shape[0], D_IN_PAD - COL_NG - N_GATE), w.dtype)
    return jnp.concatenate([w[:, o_ga:o_ga + 2 * D_MODEL], w[:, o_q:o_ng], w[:, o_hq:o_ga],
                            w[:, o_ng:o_hq], pad], axis=1)


def _moe_group(top_idx, n_tok):
    n_assign = n_tok * TOP_K
    n_blk = -(-n_assign // MOE_ROWS) + N_EXPERTS
    n_rows = n_blk * MOE_ROWS
    e_flat = top_idx.reshape(-1)
    onehot = (e_flat[:, None] == jnp.arange(N_EXPERTS, dtype=jnp.int32)[None, :]).astype(jnp.int32)
    csum = jnp.cumsum(onehot, axis=0)
    rank = jnp.sum((csum - onehot) * onehot, axis=1)
    counts = csum[-1]
    padded = (counts + MOE_ROWS - 1) // MOE_ROWS * MOE_ROWS
    pend = jnp.cumsum(padded)
    pstart = pend - padded
    dest = pstart[e_flat] + rank
    blk_e = jnp.minimum(jnp.sum(jnp.arange(n_blk)[:, None] * MOE_ROWS >= pend[None, :], axis=1), N_EXPERTS - 1)
    order = jnp.argsort(e_flat, stable=True).astype(jnp.int32)
    start = jnp.cumsum(counts) - counts
    row_e = jnp.repeat(blk_e, MOE_ROWS)
    local = jnp.arange(n_rows, dtype=jnp.int32) - pstart[row_e]
    src = order[jnp.clip(start[row_e] + local, 0, n_assign - 1)]
    row_tok = jnp.where(local < counts[row_e], src // TOP_K, n_tok)
    nused = (pend[-1] // MOE_ROWS).reshape(1)
    return row_tok, dest.reshape(n_tok, TOP_K), blk_e.astype(jnp.int32), nused.astype(jnp.int32)


def kernel(x_prompt, x_sample, cache_cmp_kv, cache_slc_kv, cache_win_kv, state_hgrn, cache_mem_kv, page_table, mem_prompt, norm_mix, w_in, cmp_pe_k, cmp_w1_k, cmp_w2_k, cmp_pe_v, cmp_w1_v, cmp_w2_v, lb_logits, hg_norm, w_branch_a, w_branch_b, w_out, norm_x, wq_x, wk_x, wv_x, wo_x, norm_ffn, w_router, b_router, w_gu, b_gu, w_dn, b_dn, norm_final):
    bsz, seq, d = x_prompt.shape
    dbs, dec_seq, _ = x_sample.shape
    past = page_table.shape[1] * PAGE_SIZE
    win_buf_len = cache_win_kv.shape[2]
    n_p = bsz * seq
    n_s = dbs * dec_seq
    pos_p = jnp.arange(seq, dtype=jnp.int32)
    pos_s = past + jnp.arange(dec_seq, dtype=jnp.int32)
    kpos_w = jnp.arange(past - win_buf_len, past + dec_seq, dtype=jnp.int32)
    lb_all = jnp.cumsum(jax.nn.softmax(lb_logits, axis=0), axis=0)
    l = 0
    cmp_par = (cmp_pe_k[l], cmp_w1_k[l], cmp_w2_k[l], cmp_pe_v[l], cmp_w1_v[l], cmp_w2_v[l])
    w_in_r = _reorder_w_in(w_in[l])
    wa, wb, wo = w_branch_a[l], w_branch_b[l], w_out[l]
    xp = x_prompt.reshape(n_p, d)
    xs = x_sample.reshape(n_s, d)

    proj_p = _norm_matmul(xp, norm_mix[l], w_in_r.astype(BF16), normalize=True, hp=False, tm=1024, tn=PROJ_TN)
    o_nsa, cmp_rows, slc_rows, win_rows = _nsa_prompt(proj_p, bsz, seq, pos_p, cmp_par)
    o_hg, s_final = _hgrn2_prompt(proj_p, bsz, seq, lb_all[l], hg_norm[l])
    xp1 = _merge(xp, o_nsa.reshape(n_p, NSA_W), o_hg, proj_p,
                 wa.astype(BF16), wb.astype(BF16), wo.astype(BF16), hp=False, tm=256)

    wkv = jnp.concatenate([wk_x[l], wv_x[l]], axis=1).astype(BF16)
    mem2d = mem_prompt.reshape(bsz * mem_prompt.shape[1], d)
    mkv = _norm_matmul(mem2d, norm_x[l], wkv, normalize=False, hp=False, tm=mem2d.shape[0], tn=PROJ_TN)
    mkv = mkv.reshape(bsz, mem_prompt.shape[1], 2 * X_W)
    xp2 = _cross(xp1.reshape(bsz, seq, d), norm_x[l], mkv, wq_x[l].astype(BF16), wo_x[l].astype(BF16), tm=512)
    xp2 = xp2.reshape(n_p, d)

    proj_s = _norm_matmul(xs, norm_mix[l], w_in_r, normalize=True, hp=True, tm=n_s, tn=PROJ_TN)
    assert dec_seq == 1
    o_nsa_s, cmp_new, slc_new, win_new = _nsa_decode(proj_s, past, cache_cmp_kv[l], cache_slc_kv[l],
                                                      cache_win_kv[l], page_table, cmp_par)
    win_keep = jnp.concatenate([cache_win_kv[l], win_new], axis=1)[:, dec_seq:]
    o_hg_s, s_new = _hgrn2_step(proj_s, state_hgrn[l], lb_all[l], hg_norm[l])
    xs1 = _merge(xs, o_nsa_s, o_hg_s, proj_s, wa, wb, wo, hp=True, tm=n_s)

    hs = _rms(xs1, norm_x[l]).reshape(dbs, dec_seq, d)
    mem_s = cache_mem_kv[l]
    qx = jnp.einsum('btd,dw->btw', hs, wq_x[l], precision=HI).reshape(dbs, dec_seq, X_HEADS, X_HD)
    sx = jnp.einsum('bthd,bmhd->bhtm', qx, mem_s[:, :, 0], precision=HI) * X_SCALE
    px = jax.nn.softmax(sx, axis=-1)
    ox = jnp.einsum('bhtm,bmhd->bthd', px, mem_s[:, :, 1], precision=HI).reshape(n_s, X_W)
    xs2 = xs1 + jnp.dot(ox, wo_x[l], precision=HI)

    n_tok = n_p + n_s
    w_r_pad = jnp.pad(w_router[l], ((0, 0), (0, LANES - N_EXPERTS)))
    b_r_pad = jnp.pad(b_router[l], (0, LANES - N_EXPERTS)).reshape(1, LANES)
    h_p, val_p, idx_p = _router(xp2, norm_ffn[l], w_r_pad, b_r_pad, tm=512)
    h_s, val_s, idx_s = _router(xs2, norm_ffn[l], w_r_pad, b_r_pad, tm=n_s)
    top_val = jnp.concatenate([val_p, val_s], axis=0)[:, :TOP_K]
    top_idx = jnp.concatenate([idx_p, idx_s], axis=0)[:, :TOP_K]
    gates = jax.nn.softmax(top_val, axis=-1)
    row_tok, pos, blk_e, nused = _moe_group(top_idx, n_tok)
    h_all = jnp.concatenate([h_p, h_s, jnp.zeros((1, d), BF16)], axis=0)
    xb = h_all[row_tok]
    yb = _moe_experts(xb, blk_e, nused, w_gu[l], b_gu[l], w_dn[l], b_dn[l])
    yg = yb[pos.T]
    y_prompt = _combine(xp2, yg, gates, norm_final, tm=256, row0=0)
    y_sample = _combine(xs2, yg, gates, norm_final, tm=n_s, row0=n_p)

    mkv_out = mkv.reshape(bsz, mem_prompt.shape[1], 2, X_HEADS, X_HD)
    return (y_prompt.reshape(bsz, seq, d), y_sample.reshape(dbs, dec_seq, d),
            cmp_rows[None], slc_rows[None], win_rows[:, seq - min(WINDOW, seq):][None], s_final[None],
            mkv_out[None], cmp_new[None], slc_new[None],
            win_keep[None], s_new[None])
```

```python
import functools
import math

import jax
import jax.numpy as jnp
from jax import lax
from jax.experimental import pallas as pl
from jax.experimental.pallas import tpu as pltpu

F32 = jnp.float32
BF16 = jnp.bfloat16
HI = lax.Precision.HIGHEST

D_MODEL = 2048
PAGE_SIZE = 128
NSA_HEADS = 8
NSA_KV = 2
NSA_HG = NSA_HEADS // NSA_KV
HD = 128
CMP_BLOCK = 32
CMP_STRIDE = 16
SLC_BLOCK = 64
SLC_TOP = 16
WINDOW = 512
QBLOCK = 128
ROPE_DIM = HD // 4
ROPE_THETA = 500000.0
SCALE = HD ** -0.5
QK_SCALE = SCALE * math.log2(math.e)
HG_HEADS = 8
HG_DK = 128
HG_DV = 128
HG_CHUNK = 64
X_HEADS = 4
X_HD = 128
X_W = X_HEADS * X_HD
X_SCALE = X_HD ** -0.5
N_EXPERTS = 32
TOP_K = 4
D_FF = D_MODEL
SWIGLU_LIMIT = 7.0
SWIGLU_ALPHA = 1.702
NORM_EPS = 1e-5
NSA_W = NSA_HEADS * HD
HG_KW = HG_HEADS * HG_DK
HG_VW = HG_HEADS * HG_DV

LANES = 128
MXU_N = 256
VMEM_LIMIT = 52 * 1024 * 1024

COL_GA = 0
COL_GB = COL_GA + D_MODEL
COL_Q = COL_GB + D_MODEL
COL_KV = COL_Q + NSA_W
COL_HQ = COL_KV + 6 * NSA_KV * HD
COL_HF = COL_HQ + HG_KW
COL_HV = COL_HF + HG_KW
COL_HGATE = COL_HV + HG_VW
COL_NG = COL_HGATE + HG_VW
N_GATE = NSA_HEADS * 3
PROJ_TN = 512
D_IN_PAD = -(-(COL_NG + N_GATE) // PROJ_TN) * PROJ_TN

MOE_ROWS = 256
MOE_TN = 1024


def _params(sem):
    return pltpu.CompilerParams(dimension_semantics=sem, vmem_limit_bytes=VMEM_LIMIT)


def _rms(x, g):
    return x * lax.rsqrt(jnp.mean(x * x, axis=-1, keepdims=True) + NORM_EPS) * g


def _dot(a, b, hp):
    if hp:
        return jnp.dot(a.astype(F32), b.astype(F32), precision=HI, preferred_element_type=F32)
    return jnp.dot(a.astype(BF16), b.astype(BF16), preferred_element_type=F32)


def _norm_matmul_kernel(x_ref, g_ref, w_ref, o_ref, h_ref, *, normalize, hp):
    @pl.when(pl.program_id(1) == 0)
    def _():
        x = x_ref[...]
        if normalize:
            x = _rms(x, g_ref[...])
        h_ref[...] = x.astype(h_ref.dtype)

    o_ref[...] = _dot(h_ref[...], w_ref[...], hp)


def _norm_matmul(x, g, w, *, normalize, hp, tm, tn):
    m, k = x.shape
    n = w.shape[1]
    return pl.pallas_call(
        functools.partial(_norm_matmul_kernel, normalize=normalize, hp=hp),
        out_shape=jax.ShapeDtypeStruct((m, n), F32),
        grid=(m // tm, n // tn),
        in_specs=[pl.BlockSpec((tm, k), lambda i, j: (i, 0)),
                  pl.BlockSpec((1, k), lambda i, j: (0, 0)),
                  pl.BlockSpec((k, tn), lambda i, j: (0, j))],
        out_specs=pl.BlockSpec((tm, tn), lambda i, j: (i, j)),
        scratch_shapes=[pltpu.VMEM((tm, k), F32 if hp else BF16)],
        compiler_params=_params(("parallel", "arbitrary")),
    )(x, g.reshape(1, k), w)


def _merge_kernel(x_ref, on_ref, oh_ref, ga_ref, gb_ref, wa_ref, wb_ref, wo_ref, o_ref, *, hp):
    a = _dot(on_ref[...], wa_ref[...], hp)
    b = _dot(oh_ref[...], wb_ref[...], hp)
    m = jax.nn.sigmoid(ga_ref[...]) * a + jax.nn.sigmoid(gb_ref[...]) * b
    o_ref[...] = x_ref[...] + _dot(m, wo_ref[...], hp)


def _merge(x, o_nsa, o_hg, proj, wa, wb, wo, *, hp, tm):
    m = x.shape[0]
    d = D_MODEL
    const = lambda i: (0, 0)
    return pl.pallas_call(
        functools.partial(_merge_kernel, hp=hp),
        out_shape=jax.ShapeDtypeStruct((m, d), F32),
        grid=(m // tm,),
        in_specs=[pl.BlockSpec((tm, d), lambda i: (i, 0)),
                  pl.BlockSpec((tm, NSA_W), lambda i: (i, 0)),
                  pl.BlockSpec((tm, HG_VW), lambda i: (i, 0)),
                  pl.BlockSpec((tm, d), lambda i: (i, COL_GA // d)),
                  pl.BlockSpec((tm, d), lambda i: (i, COL_GB // d)),
                  pl.BlockSpec((NSA_W, d), const, pipeline_mode=pl.Buffered(1)),
                  pl.BlockSpec((HG_VW, d), const, pipeline_mode=pl.Buffered(1)),
                  pl.BlockSpec((d, d), const, pipeline_mode=pl.Buffered(1))],
        out_specs=pl.BlockSpec((tm, d), lambda i: (i, 0)),
        compiler_params=_params(("parallel",)),
    )(x, o_nsa, o_hg, proj, proj, wa, wb, wo)


def _cross_kernel(x_ref, g_ref, k_ref, v_ref, wq_ref, wo_ref, o_ref):
    x = x_ref[0]
    h = _rms(x, g_ref[...])
    q = _dot(h, wq_ref[...], False)
    k = k_ref[0].astype(BF16)
    v = v_ref[0].astype(BF16)
    outs = []
    for hd in range(X_HEADS):
        sl = slice(hd * X_HD, (hd + 1) * X_HD)
        s = lax.dot_general(q[:, sl].astype(BF16), k[:, sl], (((1,), (1,)), ((), ())),
                            preferred_element_type=F32) * X_SCALE
        s = s - jnp.max(s, axis=-1, keepdims=True)
        e = jnp.exp(s)
        p = e / jnp.sum(e, axis=-1, keepdims=True)
        outs.append(jnp.dot(p.astype(BF16), v[:, sl], preferred_element_type=F32))
    o = jnp.concatenate(outs, axis=-1)
    o_ref[0] = x + _dot(o, wo_ref[...], False)


def _cross(x, g, mkv, wq, wo, *, tm):
    b, t, d = x.shape
    mlen = mkv.shape[1]
    const = lambda bi, i: (0, 0)
    return pl.pallas_call(
        _cross_kernel,
        out_shape=jax.ShapeDtypeStruct((b, t, d), F32),
        grid=(b, t // tm),
        in_specs=[pl.BlockSpec((1, tm, d), lambda bi, i: (bi, i, 0)),
                  pl.BlockSpec((1, d), const),
                  pl.BlockSpec((1, mlen, X_W), lambda bi, i: (bi, 0, 0)),
                  pl.BlockSpec((1, mlen, X_W), lambda bi, i: (bi, 0, 1)),
                  pl.BlockSpec((d, X_W), const),
                  pl.BlockSpec((X_W, d), const)],
        out_specs=pl.BlockSpec((1, tm, d), lambda bi, i: (bi, i, 0)),
        compiler_params=_params(("parallel", "parallel")),
    )(x, g.reshape(1, d), mkv, mkv, wq, wo)


def _router_kernel(x_ref, g_ref, w_ref, b_ref, h_ref, v_ref, i_ref):
    h = _rms(x_ref[...], g_ref[...])
    h_ref[...] = h.astype(BF16)
    logits = _dot(h, w_ref[...], True) + b_ref[...]
    lane = lax.broadcasted_iota(jnp.int32, logits.shape, 1)
    score = jnp.where(lane < N_EXPERTS, logits, -jnp.inf)
    vals = jnp.zeros(logits.shape, F32)
    ids = jnp.zeros(logits.shape, jnp.int32)
    for r in range(TOP_K):
        best = jnp.max(score, axis=1, keepdims=True)
        idx = jnp.min(jnp.where(score == best, lane, LANES), axis=1, keepdims=True)
        vals = jnp.where(lane == r, best, vals)
        ids = jnp.where(lane == r, idx, ids)
        score = jnp.where(lane == idx, -jnp.inf, score)
    v_ref[...] = vals
    i_ref[...] = ids


def _router(x, g, w_r_pad, b_r_pad, *, tm):
    m, d = x.shape
    const = lambda i: (0, 0)
    row = lambda i: (i, 0)
    return pl.pallas_call(
        _router_kernel,
        out_shape=(jax.ShapeDtypeStruct((m, d), BF16), jax.ShapeDtypeStruct((m, LANES), F32),
                   jax.ShapeDtypeStruct((m, LANES), jnp.int32)),
        grid=(m // tm,),
        in_specs=[pl.BlockSpec((tm, d), row),
                  pl.BlockSpec((1, d), const),
                  pl.BlockSpec((d, LANES), const),
                  pl.BlockSpec((1, LANES), const)],
        out_specs=(pl.BlockSpec((tm, d), row), pl.BlockSpec((tm, LANES), row), pl.BlockSpec((tm, LANES), row)),
        compiler_params=_params(("parallel",)),
    )(x, g.reshape(1, d), w_r_pad, b_r_pad)


def _new_weight_tile(blk_e_ref, i):
    prev = blk_e_ref[jnp.maximum(i - 1, 0)]
    return (i == 0) | (blk_e_ref[i] != prev)


def _moe_gu_kernel(blk_e_ref, nused_ref, x_ref, w_ref, b_ref, sel_ref, o_ref, wbf_ref):
    i = pl.program_id(1)
    tn = w_ref.shape[2]

    @pl.when(_new_weight_tile(blk_e_ref, i))
    def _():
        wbf_ref[...] = w_ref[0].astype(BF16)

    @pl.when(i < nused_ref[0])
    def _():
        gu = jnp.dot(x_ref[...], wbf_ref[...], preferred_element_type=F32) + b_ref[0]
        gate = jnp.minimum(gu, SWIGLU_LIMIT)
        up = jnp.clip(gu, -SWIGLU_LIMIT, SWIGLU_LIMIT)
        up_next = pltpu.roll(up, tn - 1, axis=1)
        act = (up_next + 1.0) * gate * jax.nn.sigmoid(gate * SWIGLU_ALPHA)
        lane = lax.broadcasted_iota(jnp.int32, act.shape, 1)
        act = jnp.where(lane % 2 == 0, act, 0.0).astype(BF16)
        for c in range(tn // MXU_N):
            o_ref[:, c * LANES:(c + 1) * LANES] = jnp.dot(
                act[:, c * MXU_N:(c + 1) * MXU_N], sel_ref[...], preferred_element_type=F32).astype(BF16)

    @pl.when(i >= nused_ref[0])
    def _():
        o_ref[...] = jnp.zeros_like(o_ref)


def _moe_dn_kernel(blk_e_ref, nused_ref, a_ref, w_ref, b_ref, o_ref, wbf_ref):
    i = pl.program_id(1)

    @pl.when(_new_weight_tile(blk_e_ref, i))
    def _():
        wbf_ref[...] = w_ref[0].astype(BF16)

    @pl.when(i < nused_ref[0])
    def _():
        y = jnp.dot(a_ref[...], wbf_ref[...], preferred_element_type=F32) + b_ref[0]
        o_ref[...] = y.astype(o_ref.dtype)

    @pl.when(i >= nused_ref[0])
    def _():
        o_ref[...] = jnp.zeros_like(o_ref)


def _moe_experts(xb, blk_e, nused, w_gu, b_gu, w_dn, b_dn):
    n_rows, d = xb.shape
    n_blk = n_rows // MOE_ROWS
    tn = MOE_TN
    sel = (jnp.arange(MXU_N)[:, None] == 2 * jnp.arange(LANES)[None, :]).astype(BF16)
    act = pl.pallas_call(
        _moe_gu_kernel,
        out_shape=jax.ShapeDtypeStruct((n_rows, D_FF), BF16),
        grid_spec=pltpu.PrefetchScalarGridSpec(
            num_scalar_prefetch=2,
            grid=(2 * D_FF // tn, n_blk),
            in_specs=[pl.BlockSpec((MOE_ROWS, d), lambda j, i, be, nu: (i, 0)),
                      pl.BlockSpec((1, d, tn), lambda j, i, be, nu: (be[i], 0, j)),
                      pl.BlockSpec((1, 1, tn), lambda j, i, be, nu: (be[i], 0, j)),
                      pl.BlockSpec((MXU_N, LANES), lambda j, i, be, nu: (0, 0))],
            out_specs=pl.BlockSpec((MOE_ROWS, tn // 2), lambda j, i, be, nu: (i, j)),
            scratch_shapes=[pltpu.VMEM((d, tn), BF16)]),
        compiler_params=_params(("arbitrary", "arbitrary")),
    )(blk_e, nused, xb, w_gu, b_gu.reshape(N_EXPERTS, 1, 2 * D_FF), sel)
    return pl.pallas_call(
        _moe_dn_kernel,
        out_shape=jax.ShapeDtypeStruct((n_rows, d), BF16),
        grid_spec=pltpu.PrefetchScalarGridSpec(
            num_scalar_prefetch=2,
            grid=(d // tn, n_blk),
            in_specs=[pl.BlockSpec((MOE_ROWS, D_FF), lambda j, i, be, nu: (i, 0)),
                      pl.BlockSpec((1, D_FF, tn), lambda j, i, be, nu: (be[i], 0, j)),
                      pl.BlockSpec((1, 1, tn), lambda j, i, be, nu: (be[i], 0, j))],
            out_specs=pl.BlockSpec((MOE_ROWS, tn), lambda j, i, be, nu: (i, j)),
            scratch_shapes=[pltpu.VMEM((D_FF, tn), BF16)]),
        compiler_params=_params(("arbitrary", "arbitrary")),
    )(blk_e, nused, act, w_dn, b_dn.reshape(N_EXPERTS, 1, d))


def _combine_kernel(x_ref, y_ref, gt_ref, g_ref, o_ref):
    acc = x_ref[...]
    gt = gt_ref[...]
    for k in range(TOP_K):
        acc = acc + y_ref[k].astype(F32) * gt[:, k:k + 1]
    o_ref[...] = _rms(acc, g_ref[...])


def _combine(x, yg, gates, g, *, tm, row0):
    m, d = x.shape
    off = row0 // tm
    return pl.pallas_call(
        _combine_kernel,
        out_shape=jax.ShapeDtypeStruct((m, d), F32),
        grid=(m // tm,),
        in_specs=[pl.BlockSpec((tm, d), lambda i: (i, 0)),
                  pl.BlockSpec((TOP_K, tm, d), lambda i: (0, i + off, 0)),
                  pl.BlockSpec((tm, TOP_K), lambda i: (i + off, 0)),
                  pl.BlockSpec((1, d), lambda i: (0, 0))],
        out_specs=pl.BlockSpec((tm, d), lambda i: (i, 0)),
        compiler_params=_params(("parallel",)),
    )(x, yg, gates, g.reshape(1, d))


def _rope_tile(x, cos_t, sin_lo, sin_hi):
    half = ROPE_DIM // 2
    return x * cos_t + pltpu.roll(x, half, axis=1) * sin_hi + pltpu.roll(x, HD - half, axis=1) * sin_lo


def _nsa_prep_kernel(q_ref, cmp_ref, slc_ref, win_ref, cos_ref, slo_ref, shi_ref,
                     cmp_o, slc_o, win_o, qtc_o, qtr_o, ks_o, vst_o, kw_o, vwt_o):
    cos_t, sin_lo, sin_hi = cos_ref[...], slo_ref[...], shi_ref[...]
    cmp_o[...] = cmp_ref[...]
    for g in range(NSA_KV):
        for h in range(NSA_HG):
            col = (g * NSA_HG + h) * HD
            qh = q_ref[:, col:col + HD]
            q_rot = _rope_tile(qh, cos_t, sin_lo, sin_hi)
            qtc_o[0, g, 0, :, h * HD:(h + 1) * HD] = (qh * QK_SCALE).T.astype(BF16)
            qtr_o[0, g, 0, :, h * HD:(h + 1) * HD] = (q_rot * QK_SCALE).T.astype(BF16)
        kcol = slice(g * HD, (g + 1) * HD)
        vcol = slice((NSA_KV + g) * HD, (NSA_KV + g + 1) * HD)
        for src, rows_o, k_o, vt_o in ((slc_ref, slc_o, ks_o, vst_o), (win_ref, win_o, kw_o, vwt_o)):
            k_rot = _rope_tile(src[:, kcol], cos_t, sin_lo, sin_hi)
            v = src[:, vcol]
            rows_o[:, kcol] = k_rot
            rows_o[:, vcol] = v
            k_o[0, g, 0] = k_rot.astype(BF16)
            vt_o[0, g, 0] = v.T.astype(BF16)


def _rope_tables(pos):
    half = ROPE_DIM // 2
    inv = ROPE_THETA ** (-jnp.arange(0, ROPE_DIM, 2, dtype=F32) / ROPE_DIM)
    ang = pos.astype(F32)[:, None] * inv[None, :]
    cos, sin = jnp.cos(ang), jnp.sin(ang)
    t = pos.shape[0]
    ones = jnp.ones((t, HD - ROPE_DIM), F32)
    zeros = jnp.zeros((t, HD - ROPE_DIM), F32)
    zh = jnp.zeros((t, half), F32)
    cos_t = jnp.concatenate([cos, cos, ones], axis=1)
    sin_lo = jnp.concatenate([-sin, zh, zeros], axis=1)
    sin_hi = jnp.concatenate([zh, sin, zeros], axis=1)
    return cos_t, sin_lo, sin_hi


def _nsa_prep(proj, bsz, seq, pos):
    n = bsz * seq
    tq = QBLOCK
    nq = seq // tq
    kvw = 2 * NSA_KV * HD
    cos_t, sin_lo, sin_hi = _rope_tables(pos)
    row = lambda r: (r, 0)
    tab = lambda r: (r % nq, 0)
    blk5 = lambda r: (r // nq, 0, r % nq, 0, 0)
    assert nq % SLC_TILES == 0
    big_k = lambda r: (r // nq, 0, (r % nq) // SLC_TILES, (r % nq) % SLC_TILES, 0)
    big_vt = lambda r: (r // nq, 0, (r % nq) // SLC_TILES, 0, (r % nq) % SLC_TILES)
    qt_shape = jax.ShapeDtypeStruct((bsz, NSA_KV, nq, HD, NSA_HG * tq), BF16)
    kv_shape = jax.ShapeDtypeStruct((bsz, NSA_KV, nq, tq, HD), BF16)
    ks_shape = jax.ShapeDtypeStruct((bsz, NSA_KV, nq // SLC_TILES, SLC_TILES * tq, HD), BF16)
    vst_shape = jax.ShapeDtypeStruct((bsz, NSA_KV, nq // SLC_TILES, HD, SLC_TILES * tq), BF16)
    rows_shape = jax.ShapeDtypeStruct((n, kvw), F32)
    return pl.pallas_call(
        _nsa_prep_kernel,
        out_shape=(rows_shape, rows_shape, rows_shape, qt_shape, qt_shape, ks_shape, vst_shape, kv_shape, kv_shape),
        grid=(n // tq,),
        in_specs=[pl.BlockSpec((tq, NSA_W), lambda r: (r, COL_Q // NSA_W)),
                  pl.BlockSpec((tq, kvw), lambda r: (r, COL_KV // kvw)),
                  pl.BlockSpec((tq, kvw), lambda r: (r, COL_KV // kvw + 1)),
                  pl.BlockSpec((tq, kvw), lambda r: (r, COL_KV // kvw + 2)),
                  pl.BlockSpec((tq, HD), tab), pl.BlockSpec((tq, HD), tab), pl.BlockSpec((tq, HD), tab)],
        out_specs=(pl.BlockSpec((tq, kvw), row), pl.BlockSpec((tq, kvw), row), pl.BlockSpec((tq, kvw), row),
                   pl.BlockSpec((1, NSA_KV, 1, HD, NSA_HG * tq), blk5),
                   pl.BlockSpec((1, NSA_KV, 1, HD, NSA_HG * tq), blk5),
                   pl.BlockSpec((1, NSA_KV, 1, tq, HD), big_k), pl.BlockSpec((1, NSA_KV, 1, HD, tq), big_vt),
                   pl.BlockSpec((1, NSA_KV, 1, tq, HD), blk5), pl.BlockSpec((1, NSA_KV, 1, HD, tq), blk5)),
        compiler_params=_params(("parallel",)),
    )(proj, proj, proj, proj, cos_t, sin_lo, sin_hi)


def _compress_kernel(x_ref, pe_ref, w1_ref, w2_ref, nat_o, tr_o):
    nch = x_ref.shape[1] // CMP_STRIDE
    acc = jnp.zeros((nch, 2 * HD), F32)
    bias = [jnp.zeros((8, HD), F32) for _ in range(CMP_BLOCK // CMP_STRIDE)]
    for s in range(CMP_STRIDE):
        w = w1_ref[0, s]
        rows = x_ref[0, pl.ds(s, nch, stride=CMP_STRIDE), :]
        acc = acc + jnp.dot(rows.astype(BF16), w, preferred_element_type=F32)
        for r in range(CMP_BLOCK // CMP_STRIDE):
            pe_row = jnp.broadcast_to(pe_ref[0, r * CMP_STRIDE + s:r * CMP_STRIDE + s + 1, :], (8, HD))
            pb = jnp.dot(pe_row.astype(BF16), w, preferred_element_type=F32)
            bias[r] = bias[r] + pb[:, r * HD:(r + 1) * HD]
    first = acc[:, :HD] + bias[0][0:1]
    second = acc[:, HD:] + bias[1][0:1]
    pre = first + pltpu.roll(second, nch - 1, axis=0)
    out = jnp.dot(jax.nn.silu(pre).astype(BF16), w2_ref[0], preferred_element_type=F32)
    row = lax.broadcasted_iota(jnp.int32, out.shape, 0)
    out = jnp.where(row < nch - 1, out, 0.0)
    nat_o[0, 0] = out.astype(BF16)
    tr_o[0, 0] = out.T.astype(BF16)


def _compress_weights(cmp_par):
    pe_k, w1_k, w2_k, pe_v, w1_v, w2_v = cmp_par

    def cat(w1):
        return jnp.concatenate([w1[:CMP_STRIDE], w1[CMP_STRIDE:]], axis=2)

    pe = jnp.stack([pe_k, pe_v])
    w1 = jnp.stack([cat(w1_k), cat(w1_v)]).astype(BF16)
    w2 = jnp.stack([w2_k, w2_v]).astype(BF16)
    return pe, w1, w2


def _nsa_compress_rows(rows, cmp_par):
    bsz, length, _ = rows.shape
    nch = length // CMP_STRIDE
    ncol = 2 * NSA_KV
    pe, w1, w2 = _compress_weights(cmp_par)
    return pl.pallas_call(
        _compress_kernel,
        out_shape=(jax.ShapeDtypeStruct((bsz, ncol, nch, HD), BF16), jax.ShapeDtypeStruct((bsz, ncol, HD, nch), BF16)),
        grid=(bsz, ncol),
        in_specs=[pl.BlockSpec((1, length, HD), lambda b, c: (b, 0, c)),
                  pl.BlockSpec((1, CMP_BLOCK, HD), lambda b, c: (c // NSA_KV, 0, 0)),
                  pl.BlockSpec((1, CMP_STRIDE, HD, 2 * HD), lambda b, c: (c // NSA_KV, 0, 0, 0)),
                  pl.BlockSpec((1, HD, HD), lambda b, c: (c // NSA_KV, 0, 0))],
        out_specs=(pl.BlockSpec((1, 1, nch, HD), lambda b, c: (b, c, 0, 0)),
                   pl.BlockSpec((1, 1, HD, nch), lambda b, c: (b, c, 0, 0))),
        compiler_params=_params(("parallel", "parallel")),
    )(rows, pe, w1, w2)


def _imp_matrix_t(nch, ns):
    a = SLC_BLOCK // CMP_STRIDE
    b = CMP_BLOCK // CMP_STRIDE
    j = jnp.arange(ns)[:, None]
    n = jnp.arange(nch)[None, :]
    out = jnp.zeros((ns, nch), F32)
    for m in range(a):
        for k in range(b):
            out = out + (n == a * j + a - 1 - m - k).astype(F32)
    return out


SLC_TILES = 4


def _flash_t(k_ref, vt_ref, qt, acc_ref, hi, bias_fn):
    width = qt.shape[1]
    acc_ref[...] = jnp.zeros_like(acc_ref)

    def body(it, carry):
        m_old, l_old = carry
        s = jnp.dot(k_ref[0, 0, it], qt, preferred_element_type=F32) + bias_fn(it)
        m_new = jnp.maximum(m_old, jnp.max(s, axis=0, keepdims=True))
        m_safe = jnp.where(m_new == -jnp.inf, 0.0, m_new)
        alpha = jnp.exp2(m_old - m_safe)
        p = jnp.exp2(s - m_safe)
        l_new = alpha * l_old + jnp.sum(p, axis=0, keepdims=True)
        acc_ref[...] = alpha * acc_ref[...] + jnp.dot(vt_ref[0, 0, it], p.astype(BF16), preferred_element_type=F32)
        return m_new, l_new

    init = (jnp.full((1, width), -jnp.inf, F32), jnp.zeros((1, width), F32))
    _, l_fin = lax.fori_loop(0, hi, body, init)
    return acc_ref[...] / jnp.where(l_fin > 0, l_fin, 1.0)


def _nsa_attn_kernel(qtc_ref, qtr_ref, kc_ref, vct_ref, imp_ref, ks_ref, vst_ref, kw_ref, vwt_ref, ng_ref,
                     o_ref, acc_ref, sel_ref):
    qi = pl.program_id(2)
    tq = QBLOCK
    nch = kc_ref.shape[2]
    ns = imp_ref.shape[0]
    tile4 = lambda a: jnp.concatenate([a] * NSA_HG, axis=1)
    pos = qi * tq + lax.broadcasted_iota(jnp.int32, (1, tq), 1)

    s = jnp.dot(kc_ref[0, 0], qtc_ref[0, 0, 0], preferred_element_type=F32)
    cmp_end = lax.broadcasted_iota(jnp.int32, (nch, tq), 0) * CMP_STRIDE + (CMP_BLOCK - 1)
    mask_c = tile4(cmp_end <= pos)
    s = jnp.where(mask_c, s, -jnp.inf)
    m = jnp.max(s, axis=0, keepdims=True)
    m = jnp.where(m == -jnp.inf, 0.0, m)
    e = jnp.exp2(s - m)
    d = jnp.sum(e, axis=0, keepdims=True)
    p = e / jnp.where(d > 0, d, 1.0)
    o_c = jnp.dot(vct_ref[0, 0], p.astype(BF16), preferred_element_type=F32)
    p_grp = p[:, 0:tq]
    for h in range(1, NSA_HG):
        p_grp = p_grp + p[:, h * tq:(h + 1) * tq]
    imp = jnp.dot(imp_ref[...], p_grp, precision=HI, preferred_element_type=F32)

    blk = lax.broadcasted_iota(jnp.int32, (ns, tq), 0)
    cur = pos // SLC_BLOCK
    forced = (blk == 0) | (blk == cur) | (blk == cur - 1)
    score = jnp.where(forced, jnp.inf, imp)
    score = jnp.where(blk <= cur, score, -jnp.inf)
    sel = jnp.zeros((ns, tq), F32)
    for _ in range(min(SLC_TOP, ns)):
        best = jnp.max(score, axis=0, keepdims=True)
        idx = jnp.min(jnp.where(score == best, blk, ns), axis=0, keepdims=True)
        pick = blk == idx
        sel = jnp.where(pick, 1.0, sel)
        score = jnp.where(pick, -jnp.inf, score)
    sel_ref[...] = sel

    big = SLC_TILES * tq
    per_step = big // SLC_BLOCK
    key_in_step = lax.broadcasted_iota(jnp.int32, (big, tq), 0)

    def bias_slc(it):
        chosen = jnp.concatenate([jnp.broadcast_to(sel_ref[pl.ds(it * per_step + b, 1), :], (SLC_BLOCK, tq))
                                  for b in range(per_step)], axis=0)
        ok = (chosen > 0.5) & (it * big + key_in_step <= pos)
        return tile4(jnp.where(ok, 0.0, -jnp.inf))

    qtr = qtr_ref[0, 0, 0]
    o_s = _flash_t(ks_ref, vst_ref, qtr, acc_ref, qi // SLC_TILES + 1, bias_slc)

    key_in_tile = lax.broadcasted_iota(jnp.int32, (tq, tq), 0)
    win_tiles = [qi - WINDOW // tq + j for j in range(WINDOW // tq + 1)]
    biases = []
    for kt in win_tiles:
        key_pos = kt * tq + key_in_tile
        delta = pos - key_pos
        biases.append(jnp.where((delta >= 0) & (delta <= WINDOW) & (key_pos >= 0), 0.0, -jnp.inf))
    k_win = jnp.concatenate([kw_ref[0, 0, jnp.maximum(kt, 0)] for kt in win_tiles], axis=0)
    s = jnp.dot(k_win, qtr, preferred_element_type=F32) + tile4(jnp.concatenate(biases, axis=0))
    e = jnp.exp2(s - jnp.max(s, axis=0, keepdims=True))
    d = jnp.sum(e, axis=0, keepdims=True)
    o_w = jnp.zeros_like(o_c)
    for j, kt in enumerate(win_tiles):
        o_w = o_w + jnp.dot(vwt_ref[0, 0, jnp.maximum(kt, 0)], e[j * tq:(j + 1) * tq].astype(BF16),
                            preferred_element_type=F32)
    o_w = o_w / d

    gates = jax.nn.sigmoid(ng_ref[0, 0])
    for h in range(NSA_HG):
        gate = gates[3 * h:3 * h + 3]
        sl = slice(h * tq, (h + 1) * tq)
        o = gate[0:1] * o_c[:, sl] + gate[1:2] * o_s[:, sl] + gate[2:3] * o_w[:, sl]
        o_ref[0, :, h * HD:(h + 1) * HD] = o.T.astype(o_ref.dtype)


def _nsa_attn(qtc, qtr, kc_nat, kc_tr, ks5, vst5, kw5, vwt5, ng_t):
    bsz, _, nq, _, width = qtc.shape
    tq = QBLOCK
    seq = nq * tq
    nch = kc_nat.shape[2]
    ns = seq // SLC_BLOCK
    imp_t = _imp_matrix_t(nch, ns)
    q_spec = pl.BlockSpec((1, 1, 1, HD, width), lambda b, g, i: (b, g, i, 0, 0))
    k5_spec = pl.BlockSpec((1, 1, nq, tq, HD), lambda b, g, i: (b, g, 0, 0, 0))
    whole = lambda a: pl.BlockSpec((1, 1) + a.shape[2:], lambda b, g, i: (b, g, 0, 0, 0))
    return pl.pallas_call(
        _nsa_attn_kernel,
        out_shape=jax.ShapeDtypeStruct((bsz, seq, NSA_W), BF16),
        grid=(bsz, NSA_KV, nq),
        in_specs=[q_spec, q_spec,
                  pl.BlockSpec((1, 1, nch, HD), lambda b, g, i: (b, g, 0, 0)),
                  pl.BlockSpec((1, 1, HD, nch), lambda b, g, i: (b, NSA_KV + g, 0, 0)),
                  pl.BlockSpec((ns, nch), lambda b, g, i: (0, 0)),
                  whole(ks5), whole(vst5), k5_spec, k5_spec,
                  pl.BlockSpec((1, 1, N_GATE // NSA_KV, tq), lambda b, g, i: (b, g, 0, i))],
        out_specs=pl.BlockSpec((1, tq, NSA_HG * HD), lambda b, g, i: (b, i, g)),
        scratch_shapes=[pltpu.VMEM((HD, width), F32), pltpu.VMEM((ns, tq), F32)],
        compiler_params=_params(("parallel", "parallel", "arbitrary")),
    )(qtc, qtr, kc_nat, kc_tr, imp_t, ks5, vst5, kw5, vwt5, ng_t)


def _nsa_prompt(proj, bsz, seq, pos, cmp_par):
    cmp_rows, slc_rows, win_rows, qtc, qtr, ks5, vst5, kw5, vwt5 = _nsa_prep(proj, bsz, seq, pos)
    kc_nat, kc_tr = _nsa_compress_rows(cmp_rows.reshape(bsz, seq, -1), cmp_par)
    ng_t = proj[:, COL_NG:COL_NG + N_GATE].reshape(bsz, seq, NSA_KV, N_GATE // NSA_KV).transpose(0, 2, 3, 1)
    o_nsa = _nsa_attn(qtc, qtr, kc_nat, kc_tr, ks5, vst5, kw5, vwt5, ng_t)
    shape5 = (bsz, seq, 2, NSA_KV, HD)
    return o_nsa, cmp_rows.reshape(shape5), slc_rows.reshape(shape5), win_rows.reshape(shape5)


HG_TILE = 128
HG_DIAG = 8


def _split3(x):
    hi = x.astype(BF16)
    r1 = x - hi.astype(F32)
    mid = r1.astype(BF16)
    lo = (r1 - mid.astype(F32)).astype(BF16)
    return hi, mid, lo


def _hgrn2_kernel(hq_ref, hf_ref, hv_ref, hg_ref, lb_ref, gn_ref, o_ref, s_ref, st_ref):
    i = pl.program_id(2)
    c = HG_TILE

    @pl.when(i == 0)
    def _():
        st_ref[...] = jnp.zeros_like(st_ref)

    lb = lb_ref[0]
    q = hq_ref[...]
    f = lb + (1.0 - lb) * jax.nn.sigmoid(hf_ref[...])
    logf = jnp.log(f)
    k = 1.0 - f
    v = hv_ref[...]
    row = lax.broadcasted_iota(jnp.int32, (c, c), 0)
    col = lax.broadcasted_iota(jnp.int32, (c, c), 1)
    tri = (row >= col).astype(BF16)
    b = sum(jnp.dot(tri, part, preferred_element_type=F32) for part in _split3(logf))
    b_last = b[c - 1:c, :]

    o = (jnp.sum(q * k, axis=1, keepdims=True)) * v
    for delta in range(1, HG_DIAG):
        valid = (row % HG_DIAG) >= delta
        decay = jnp.exp(jnp.where(valid, b - pltpu.roll(b, delta, axis=0), -jnp.inf))
        a = jnp.sum(q * pltpu.roll(k, delta, axis=0) * decay, axis=1, keepdims=True)
        o = o + a * pltpu.roll(v, delta, axis=0)

    a_mat = jnp.zeros((c, c), F32)
    blk = 2 * HG_DIAG
    while blk <= c:
        half = blk // 2
        b_mid = jnp.concatenate(
            [jnp.broadcast_to(b[s + half - 1:s + half, :], (blk, b.shape[1])) for s in range(0, c, blk)], axis=0)
        second = (row % blk) >= half
        qs = q * jnp.exp(jnp.where(second, b - b_mid, -jnp.inf))
        ks = k * jnp.exp(jnp.where(second, -jnp.inf, b_mid - b))
        a_blk = lax.dot_general(qs.astype(BF16), ks.astype(BF16), (((1,), (1,)), ((), ())),
                                preferred_element_type=F32)
        a_mat = a_mat + jnp.where((row // blk) == (col // blk), a_blk, 0.0)
        blk *= 2

    st = st_ref[...]
    o = o + jnp.dot(a_mat.astype(BF16), v.astype(BF16), preferred_element_type=F32)
    o = o + lax.dot_general((q * jnp.exp(b)).astype(BF16), st.astype(BF16), (((1,), (1,)), ((), ())),
                            preferred_element_type=F32)
    k_end = (k * jnp.exp(b_last - b)).astype(BF16)
    st_new = st * jnp.exp(b_last) + jnp.dot(v.T.astype(BF16), k_end, preferred_element_type=F32)
    st_ref[...] = st_new

    o = o * lax.rsqrt(jnp.mean(o * o, axis=1, keepdims=True) + NORM_EPS) * gn_ref[0]
    o_ref[...] = (o * jax.nn.silu(hg_ref[...])).astype(o_ref.dtype)

    @pl.when(i == pl.num_programs(2) - 1)
    def _():
        s_ref[0, 0] = st_new.T


def _hgrn2_prompt(proj, bsz, seq, lb, g_norm):
    c = HG_TILE
    nc = seq // c
    col = lambda base: (lambda b, h, i: (b * nc + i, base // HG_DK + h))
    vec = pl.BlockSpec((1, 1, HG_DK), lambda b, h, i: (h, 0, 0))
    return pl.pallas_call(
        _hgrn2_kernel,
        out_shape=(jax.ShapeDtypeStruct((bsz * seq, HG_VW), BF16),
                   jax.ShapeDtypeStruct((bsz, HG_HEADS, HG_DK, HG_DV), F32)),
        grid=(bsz, HG_HEADS, nc),
        in_specs=[pl.BlockSpec((c, HG_DK), col(COL_HQ)), pl.BlockSpec((c, HG_DK), col(COL_HF)),
                  pl.BlockSpec((c, HG_DV), col(COL_HV)), pl.BlockSpec((c, HG_DV), col(COL_HGATE)), vec, vec],
        out_specs=(pl.BlockSpec((c, HG_DV), lambda b, h, i: (b * nc + i, h)),
                   pl.BlockSpec((1, 1, HG_DK, HG_DV), lambda b, h, i: (b, h, 0, 0))),
        scratch_shapes=[pltpu.VMEM((HG_DV, HG_DK), F32)],
        compiler_params=_params(("parallel", "parallel", "arbitrary")),
    )(proj, proj, proj, proj, lb.reshape(HG_HEADS, 1, HG_DK), g_norm.reshape(HG_HEADS, 1, HG_DV))


PAGES_PER_STEP = 32


def _paged_compress_kernel(pt_ref, *refs):
    pages = refs[:PAGES_PER_STEP]
    wh_ref, wl_ref = refs[PAGES_PER_STEP], refs[PAGES_PER_STEP + 1]
    o_ref = refs[PAGES_PER_STEP + 2]
    per_page = PAGE_SIZE // CMP_STRIDE
    ncol = 2 * NSA_KV
    for c in range(ncol):
        acc = jnp.zeros((PAGES_PER_STEP * per_page, 2 * HD), F32)
        for s2 in range(CMP_STRIDE // 2):
            def rows(s):
                return jnp.concatenate(
                    [p[0, pl.ds(s * ncol + c, per_page, stride=CMP_STRIDE * ncol), :] for p in pages], axis=0)
            lhs = jnp.concatenate([rows(2 * s2), rows(2 * s2 + 1)], axis=1)
            hi = lhs.astype(BF16)
            lo = (lhs - hi.astype(F32)).astype(BF16)
            wh = wh_ref[c // NSA_KV, s2]
            acc = acc + jnp.dot(hi, wh, preferred_element_type=F32) + jnp.dot(lo, wh, preferred_element_type=F32)
            acc = acc + jnp.dot(hi, wl_ref[c // NSA_KV, s2], preferred_element_type=F32)
        o_ref[0, c] = acc


def _paged_compress(cache, page_table, w1cat):
    dbs, n_pages = page_table.shape
    per_page = PAGE_SIZE // CMP_STRIDE
    w_st = w1cat.reshape(2, CMP_STRIDE // 2, 2 * HD, 2 * HD)
    w_hi = w_st.astype(BF16)
    w_lo = (w_st - w_hi.astype(F32)).astype(BF16)
    w_spec = pl.BlockSpec((2, CMP_STRIDE // 2, 2 * HD, 2 * HD), lambda b, s, pt: (0, 0, 0, 0))
    page_spec = lambda k: pl.BlockSpec((1, PAGE_SIZE * 2 * NSA_KV, HD),
                                       lambda b, s, pt: (pt[b, s * PAGES_PER_STEP + k], 0, 0))
    return pl.pallas_call(
        _paged_compress_kernel,
        out_shape=jax.ShapeDtypeStruct((dbs, 2 * NSA_KV, n_pages * per_page, 2 * HD), F32),
        grid_spec=pltpu.PrefetchScalarGridSpec(
            num_scalar_prefetch=1,
            grid=(dbs, n_pages // PAGES_PER_STEP),
            in_specs=[page_spec(k) for k in range(PAGES_PER_STEP)] + [w_spec, w_spec],
            out_specs=pl.BlockSpec((1, 2 * NSA_KV, PAGES_PER_STEP * per_page, 2 * HD), lambda b, s, pt: (b, 0, s, 0))),
        compiler_params=_params(("parallel", "arbitrary")),
    )(page_table, *([cache] * PAGES_PER_STEP), w_hi, w_lo)


def _dec_select_kernel(p_ref, bias_ref, w2_ref, q_ref, imp_ref, oc_ref, idx_ref, *, pos):
    nch = p_ref.shape[2]
    ns_pad = imp_ref.shape[1]
    ns = pos // SLC_BLOCK + 1
    blocks = []
    for c in range(2 * NSA_KV):
        part = p_ref[0, c]
        pre = part[:, :HD] + pltpu.roll(part[:, HD:], nch - 1, axis=0) + bias_ref[c // NSA_KV]
        out = jnp.dot(jax.nn.silu(pre), w2_ref[c // NSA_KV], precision=HI, preferred_element_type=F32)
        blocks.append(out)
    q = q_ref[0]
    n_col = lax.broadcasted_iota(jnp.int32, (nch, 1), 0)
    valid = (n_col * CMP_STRIDE + (CMP_BLOCK - 1) <= pos) & (n_col < nch - 1)
    lane = lax.broadcasted_iota(jnp.int32, (1, ns_pad), 1)
    out_lane = lax.broadcasted_iota(jnp.int32, (NSA_HEADS, LANES), 1)
    out_row = lax.broadcasted_iota(jnp.int32, (NSA_HEADS, LANES), 0)
    head_row = lax.broadcasted_iota(jnp.int32, (NSA_HEADS, HD), 0)
    cur = pos // SLC_BLOCK
    o_c = jnp.zeros((NSA_HEADS, HD), F32)
    idx_out = jnp.zeros((NSA_HEADS, LANES), jnp.int32)
    for g in range(NSA_KV):
        kc = blocks[g]
        vc = jnp.where(valid, blocks[NSA_KV + g], 0.0)
        p_grp = jnp.zeros((nch, 1), F32)
        for h in range(NSA_HG):
            hh = g * NSA_HG + h
            s = jnp.sum(kc * q[hh:hh + 1, :], axis=1, keepdims=True) * SCALE
            s = jnp.where(valid, s, -jnp.inf)
            m = jnp.max(s, axis=0, keepdims=True)
            m = jnp.where(m == -jnp.inf, 0.0, m)
            e = jnp.exp(s - m)
            d = jnp.sum(e, axis=0, keepdims=True)
            p = e / jnp.where(d > 0, d, 1.0)
            o_c = jnp.where(head_row == hh, jnp.sum(p * vc, axis=0, keepdims=True), o_c)
            p_grp = p_grp + p
        imp = jnp.sum(p_grp * imp_ref[...], axis=0, keepdims=True)
        forced = (lane == 0) | (lane == cur) | (lane == cur - 1)
        score = jnp.where(forced, jnp.inf, imp)
        score = jnp.where(lane < ns, score, -jnp.inf)
        for r in range(min(SLC_TOP, ns)):
            best = jnp.max(score, axis=1, keepdims=True)
            idx = jnp.min(jnp.where(score == best, lane, ns_pad), axis=1, keepdims=True)
            score = jnp.where(lane == idx, -jnp.inf, score)
            idx_out = jnp.where((out_lane == r) & (out_row == g), idx, idx_out)
    oc_ref[0] = o_c
    idx_ref[0] = idx_out


def _dec_select(part, bias, w2, q, pos):
    dbs, ncol, nch, _ = part.shape
    ns = pos // SLC_BLOCK + 1
    ns_pad = -(-ns // LANES) * LANES
    imp = jnp.pad(_imp_matrix_t(nch, ns).T, ((0, 0), (0, ns_pad - ns)))
    return pl.pallas_call(
        functools.partial(_dec_select_kernel, pos=pos),
        out_shape=(jax.ShapeDtypeStruct((dbs, NSA_HEADS, HD), F32),
                   jax.ShapeDtypeStruct((dbs, NSA_HEADS, LANES), jnp.int32)),
        grid=(dbs,),
        in_specs=[pl.BlockSpec((1, ncol, nch, 2 * HD), lambda b: (b, 0, 0, 0)),
                  pl.BlockSpec((2, 1, HD), lambda b: (0, 0, 0)),
                  pl.BlockSpec((2, HD, HD), lambda b: (0, 0, 0)),
                  pl.BlockSpec((1, NSA_HEADS, HD), lambda b: (b, 0, 0)),
                  pl.BlockSpec((nch, ns_pad), lambda b: (0, 0))],
        out_specs=(pl.BlockSpec((1, NSA_HEADS, HD), lambda b: (b, 0, 0)),
                   pl.BlockSpec((1, NSA_HEADS, LANES), lambda b: (b, 0, 0))),
        compiler_params=_params(("parallel",)),
    )(part, bias, w2, q, imp)


def _dec_attend(q, key_tiles, val_tiles, valids, k_new, v_new):
    head_row = lax.broadcasted_iota(jnp.int32, q.shape, 0)
    out = jnp.zeros(q.shape, F32)
    for h in range(q.shape[0]):
        qh = q[h:h + 1, :]
        scores = []
        for kt, ok in zip(key_tiles, valids):
            s = jnp.sum(kt * qh, axis=1, keepdims=True) * SCALE
            scores.append(s if ok is None else jnp.where(ok, s, -jnp.inf))
        s_new = jnp.sum(qh * k_new, axis=1, keepdims=True) * SCALE
        m = s_new
        for s in scores:
            m = jnp.maximum(m, jnp.max(s, axis=0, keepdims=True))
        e_new = jnp.exp(s_new - m)
        den = e_new
        acc = e_new * v_new
        for s, vt in zip(scores, val_tiles):
            e = jnp.exp(s - m)
            den = den + jnp.sum(e, axis=0, keepdims=True)
            acc = acc + jnp.sum(e * vt, axis=0, keepdims=True)
        out = jnp.where(head_row == h, acc / den, out)
    return out


def _dec_slc_win_kernel(pt_ref, idx_ref, *refs, last_block):
    nsel = SLC_TOP
    blks = refs[:nsel]
    q_ref, snew_ref, win_ref, wnew_ref, oc_ref, ng_ref, o_ref = refs[nsel:]
    b, g = pl.program_id(0), pl.program_id(1)
    ncol = 2 * NSA_KV

    def column(r, c0):
        out = r[0, pl.ds(c0, r.shape[1] // ncol, stride=ncol), :]
        for gg in range(1, NSA_KV):
            out = jnp.where(g == gg, r[0, pl.ds(c0 + gg, r.shape[1] // ncol, stride=ncol), :], out)
        return out

    keys = lambda r: column(r, 0)
    vals = lambda r: column(r, NSA_KV)
    q = q_ref[0, 0]
    base = (b * NSA_KV + g) * nsel
    valids = [idx_ref[base + k] != last_block for k in range(nsel)]
    o_s = _dec_attend(q, [keys(r) for r in blks], [vals(r) for r in blks], valids,
                      snew_ref[0, 0, pl.ds(g, 1), :], snew_ref[0, 1, pl.ds(g, 1), :])
    o_w = _dec_attend(q, [keys(win_ref)], [vals(win_ref)], [None],
                      wnew_ref[0, 0, pl.ds(g, 1), :], wnew_ref[0, 1, pl.ds(g, 1), :])
    gate = jax.nn.sigmoid(ng_ref[0, 0])
    o_ref[0, 0] = gate[:, 0:1] * oc_ref[0, 0] + gate[:, 1:2] * o_s + gate[:, 2:3] * o_w


def _dec_slc_win(slc_cache, page_table, sel_idx, q_rot, slc_new, win_cache, win_new, o_c, ng, *, last_block):
    dbs = q_rot.shape[0]
    per_page = PAGE_SIZE // SLC_BLOCK
    ncol = 2 * NSA_KV
    halves = slc_cache.reshape(slc_cache.shape[0] * per_page, SLC_BLOCK * ncol, HD)
    wrows = win_cache.shape[1]

    def blk_spec(k):
        def index(b, g, pt, idx):
            j = jnp.minimum(idx[(b * NSA_KV + g) * SLC_TOP + k], last_block - 1)
            return (pt[b * (page_table.shape[1]) + j // per_page] * per_page + j % per_page, 0, 0)
        return pl.BlockSpec((1, SLC_BLOCK * ncol, HD), index)

    grp4 = pl.BlockSpec((1, 1, NSA_HG, HD), lambda b, g, pt, idx: (b, g, 0, 0))
    new4 = pl.BlockSpec((1, 2, NSA_KV, HD), lambda b, g, pt, idx: (b, 0, 0, 0))
    return pl.pallas_call(
        functools.partial(_dec_slc_win_kernel, last_block=last_block),
        out_shape=jax.ShapeDtypeStruct((dbs, NSA_KV, NSA_HG, HD), F32),
        grid_spec=pltpu.PrefetchScalarGridSpec(
            num_scalar_prefetch=2,
            grid=(dbs, NSA_KV),
            in_specs=[blk_spec(k) for k in range(SLC_TOP)]
            + [grp4, new4,
               pl.BlockSpec((1, wrows, HD), lambda b, g, pt, idx: (b, 0, 0)),
               new4, grp4,
               pl.BlockSpec((1, 1, NSA_HG, 3), lambda b, g, pt, idx: (b, g, 0, 0))],
            out_specs=grp4),
        compiler_params=_params(("parallel", "parallel")),
    )(page_table.reshape(-1), sel_idx, *([halves] * SLC_TOP), q_rot, slc_new, win_cache, win_new, o_c, ng)


def _hgrn2_step_kernel(hq_ref, hf_ref, hv_ref, hg_ref, lb_ref, gn_ref, s_ref, o_ref, so_ref):
    dbs = hq_ref.shape[0]
    lb = lb_ref[0]
    f_all = lb + (1.0 - lb) * jax.nn.sigmoid(hf_ref[...])
    row = lax.broadcasted_iota(jnp.int32, (HG_DK, HG_DK), 0)
    col = lax.broadcasted_iota(jnp.int32, (HG_DK, HG_DK), 1)
    diag = row == col

    def as_col(v_row):
        return jnp.sum(jnp.where(diag, v_row, 0.0), axis=1, keepdims=True)

    for b in range(dbs):
        f = f_all[b:b + 1]
        s_new = as_col(f) * s_ref[b, 0] + as_col(1.0 - f) * hv_ref[b:b + 1, :]
        so_ref[b, 0] = s_new
        o_ref[b:b + 1, :] = jnp.sum(s_new * as_col(hq_ref[b:b + 1, :]), axis=0, keepdims=True)
    o = o_ref[...]
    o = o * lax.rsqrt(jnp.mean(o * o, axis=1, keepdims=True) + NORM_EPS) * gn_ref[0]
    o_ref[...] = o * jax.nn.silu(hg_ref[...])


def _hgrn2_step(proj, state, lb, g_norm):
    dbs = proj.shape[0]
    col = lambda base: (lambda h: (0, base // HG_DK + h))
    vec = pl.BlockSpec((1, 1, HG_DK), lambda h: (h, 0, 0))
    st = pl.BlockSpec((dbs, 1, HG_DK, HG_DV), lambda h: (0, h, 0, 0))
    return pl.pallas_call(
        _hgrn2_step_kernel,
        out_shape=(jax.ShapeDtypeStruct((dbs, HG_VW), F32), jax.ShapeDtypeStruct(state.shape, F32)),
        grid=(HG_HEADS,),
        in_specs=[pl.BlockSpec((dbs, HG_DK), col(COL_HQ)), pl.BlockSpec((dbs, HG_DK), col(COL_HF)),
                  pl.BlockSpec((dbs, HG_DV), col(COL_HV)), pl.BlockSpec((dbs, HG_DV), col(COL_HGATE)),
                  vec, vec, st],
        out_specs=(pl.BlockSpec((dbs, HG_DV), lambda h: (0, h)), st),
        compiler_params=_params(("parallel",)),
    )(proj, proj, proj, proj, lb.reshape(HG_HEADS, 1, HG_DK), g_norm.reshape(HG_HEADS, 1, HG_DV), state)


def _nsa_decode(proj, pos, cache_cmp, cache_slc, cache_win, page_table, cmp_par):
    dbs = proj.shape[0]
    n_pool = cache_cmp.shape[0]
    assert pos % SLC_BLOCK == 0 and pos % CMP_STRIDE == 0 and pos == page_table.shape[1] * PAGE_SIZE
    assert cache_win.shape[1] <= WINDOW
    pos_arr = jnp.full((1,), pos, jnp.int32)
    q = proj[:, COL_Q:COL_Q + NSA_W].reshape(dbs, 1, NSA_HEADS, HD)
    q_rot = _rope_partial(q, pos_arr)
    kv6 = proj[:, COL_KV:COL_KV + 6 * NSA_KV * HD].reshape(dbs, 1, 6, NSA_KV, HD)
    cmp_new = kv6[:, :, 0:2]
    slc_new = jnp.stack([_rope_partial(kv6[:, :, 2], pos_arr), kv6[:, :, 3]], axis=2)
    win_new = jnp.stack([_rope_partial(kv6[:, :, 4], pos_arr), kv6[:, :, 5]], axis=2)
    pe_k, w1_k, _, pe_v, w1_v, _ = cmp_par
    w1cat = jnp.stack([jnp.concatenate([w[:CMP_STRIDE], w[CMP_STRIDE:]], axis=2) for w in (w1_k, w1_v)])
    bias = jnp.stack([jnp.einsum('sd,sdh->h', pe_k, w1_k, precision=HI),
                      jnp.einsum('sd,sdh->h', pe_v, w1_v, precision=HI)]).reshape(2, 1, HD)
    part = _paged_compress(cache_cmp.reshape(n_pool, PAGE_SIZE * 2 * NSA_KV, HD), page_table, w1cat)
    w2f = jnp.stack([cmp_par[2], cmp_par[5]])
    o_c, idx = _dec_select(part, bias, w2f, q.reshape(dbs, NSA_HEADS, HD), pos)
    sel_idx = idx[:, :NSA_KV, :SLC_TOP].reshape(-1)
    ng = proj[:, COL_NG:COL_NG + N_GATE].reshape(dbs, NSA_KV, NSA_HG, 3)
    o_nsa = _dec_slc_win(cache_slc.reshape(n_pool, PAGE_SIZE * 2 * NSA_KV, HD), page_table, sel_idx,
                         q_rot.reshape(dbs, NSA_KV, NSA_HG, HD), slc_new.reshape(dbs, 2, NSA_KV, HD),
                         cache_win.reshape(dbs, cache_win.shape[1] * 2 * NSA_KV, HD),
                         win_new.reshape(dbs, 2, NSA_KV, HD),
                         o_c.reshape(dbs, NSA_KV, NSA_HG, HD), ng, last_block=pos // SLC_BLOCK)
    return o_nsa.reshape(dbs, NSA_W), cmp_new, slc_new, win_new


def _rope_partial(x, pos):
    half = ROPE_DIM // 2
    inv = ROPE_THETA ** (-jnp.arange(0, ROPE_DIM, 2, dtype=F32) / ROPE_DIM)
    ang = pos.astype(F32)[:, None] * inv[None, :]
    cos = jnp.cos(ang)[:, None, :]
    sin = jnp.sin(ang)[:, None, :]
    xr = x[..., :ROPE_DIM]
    x1, x2 = xr[..., :half], xr[..., half:]
    rot = jnp.concatenate([x1 * cos - x2 * sin, x2 * cos + x1 * sin], axis=-1)
    return jnp.concatenate([rot, x[..., ROPE_DIM:]], axis=-1)


def _masked_softmax(s, mask):
    s = jnp.where(mask, s, -jnp.inf)
    m = jnp.max(s, axis=-1, keepdims=True)
    m = jnp.where(jnp.isfinite(m), m, 0.0)
    e = jnp.where(mask, jnp.exp(s - m), 0.0)
    d = jnp.sum(e, axis=-1, keepdims=True)
    return e / jnp.where(d > 0, d, 1.0)


def _nsa_compress(rows, pe, w1, w2, prec):
    b, l = rows.shape[:2]
    n_sub = CMP_BLOCK // CMP_STRIDE
    nc = (l - CMP_BLOCK) // CMP_STRIDE + 1
    nch = l // CMP_STRIDE
    ch = rows[:, :nch * CMP_STRIDE].reshape(b, nch, CMP_STRIDE, NSA_KV, HD)
    acc = 0.0
    for r in range(n_sub):
        sl = slice(r * CMP_STRIDE, (r + 1) * CMP_STRIDE)
        pr = jnp.einsum('bnsgd,sdh->bngh', ch + pe[sl][:, None, :], w1[sl], precision=prec)
        acc = acc + pr[:, r:r + nc]
    return jnp.einsum('bngh,hd->bngd', jax.nn.silu(acc), w2, precision=prec)


def _cmp_to_slc(p_cmp, ns):
    a = SLC_BLOCK // CMP_STRIDE
    b = CMP_BLOCK // CMP_STRIDE
    nc = p_cmp.shape[-1]
    total = a * ns + a + b
    pad_cfg = [(0, 0)] * (p_cmp.ndim - 1) + [(b - 1, total - (b - 1) - nc)]
    pp = jnp.pad(p_cmp, pad_cfg)
    out = 0.0
    for m in range(a):
        for n in range(b):
            st = a - 1 - m - n + (b - 1)
            out = out + pp[..., st:st + a * ns:a]
    return out


def _nsa_cmp_slc(q, q_rot, pos, kc, vc, ks_rows, vs_rows, prec):
    b, t = q.shape[:2]
    l = ks_rows.shape[1]
    nc = kc.shape[1]
    ns = -(-l // SLC_BLOCK)
    nsel = min(SLC_TOP, ns)
    cmp_end = jnp.arange(nc) * CMP_STRIDE + CMP_BLOCK - 1
    blk_ids = jnp.arange(ns)
    key_pos = jnp.arange(l)
    key_blk = key_pos // SLC_BLOCK

    def one_block(args):
        qr, qo, ps = args
        s = jnp.einsum('bqghd,bngd->bqghn', qr, kc, precision=prec) * SCALE
        mc = (cmp_end[None, :] <= ps[:, None])[None, :, None, None, :]
        p = _masked_softmax(s, mc)
        o_c = jnp.einsum('bqghn,bngd->bqghd', p, vc, precision=prec)
        imp = _cmp_to_slc(jnp.sum(p, axis=3), ns)
        cur = ps // SLC_BLOCK
        vis = blk_ids[None, :] * SLC_BLOCK <= ps[:, None]
        forced = (blk_ids[None, :] == 0) | (blk_ids[None, :] == cur[:, None]) | (blk_ids[None, :] == cur[:, None] - 1)
        score = jnp.where(forced[None, :, None, :], jnp.inf, imp)
        score = jnp.where(vis[None, :, None, :], score, -jnp.inf)
        _, idx = lax.top_k(score, nsel)
        sel = jnp.any(idx[..., None] == blk_ids, axis=-2)
        ms = sel[..., key_blk] & (key_pos[None, :] <= ps[:, None])[None, :, None, :]
        ss = jnp.einsum('bqghd,bkgd->bqghk', qo, ks_rows, precision=prec) * SCALE
        pss = _masked_softmax(ss, ms[:, :, :, None, :])
        o_s = jnp.einsum('bqghk,bkgd->bqghd', pss, vs_rows, precision=prec)
        return o_c, o_s

    qb = math.gcd(t, QBLOCK)
    nqb = t // qb

    def split_q(a):
        return a.reshape(b, nqb, qb, NSA_KV, NSA_HG, HD).transpose(1, 0, 2, 3, 4, 5)

    def merge_q(a):
        return a.transpose(1, 0, 2, 3, 4, 5).reshape(b, t, NSA_HEADS, HD)

    o_c, o_s = lax.map(one_block, (split_q(q), split_q(q_rot), pos.reshape(nqb, qb)))
    return merge_q(o_c), merge_q(o_s)


def _nsa_window_banded(q_rot, win_rows, pos):
    b, t = q_rot.shape[:2]
    qb = math.gcd(t, QBLOCK)
    nb = t // qb
    span = qb + WINDOW
    rows = jnp.pad(win_rows, ((0, 0), (WINDOW, 0), (0, 0), (0, 0), (0, 0)))
    idx = jnp.arange(nb)[:, None] * qb + jnp.arange(span)[None, :]
    kv = rows[:, idx]
    kpos = idx - WINDOW
    delta = pos.reshape(nb, qb)[:, :, None] - kpos[:, None, :]
    mask = (delta >= 0) & (delta <= WINDOW) & (kpos[:, None, :] >= 0)
    q = q_rot.reshape(b, nb, qb, NSA_KV, NSA_HG, HD)
    s = jnp.einsum('bnqghd,bnkgd->bnqghk', q, kv[:, :, :, 0]) * SCALE
    p = _masked_softmax(s, mask[None, :, :, None, None, :])
    o = jnp.einsum('bnqghk,bnkgd->bnqghd', p, kv[:, :, :, 1])
    return o.reshape(b, t, NSA_HEADS, HD)


def _nsa_window_buffer(q_rot, win_all, qpos, kpos, prec):
    b, t = q_rot.shape[:2]
    q = q_rot.reshape(b, t, NSA_KV, NSA_HG, HD)
    s = jnp.einsum('bqghd,bkgd->bqghk', q, win_all[:, :, 0], precision=prec) * SCALE
    delta = qpos[:, None] - kpos[None, :]
    mask = ((delta >= 0) & (delta <= WINDOW))[None, :, None, None, :]
    p = _masked_softmax(s, mask)
    o = jnp.einsum('bqghk,bkgd->bqghd', p, win_all[:, :, 1], precision=prec)
    return o.reshape(b, t, NSA_HEADS, HD)


def _nsa_combine(q, q_rot, pos, cmp_rows, slc_rows, o_win, gate_logits, cmp_par, prec):
    b, t = q.shape[:2]
    pe_k, w1_k, w2_k, pe_v, w1_v, w2_v = cmp_par
    kc = _nsa_compress(cmp_rows[:, :, 0], pe_k, w1_k, w2_k, None)
    vc = _nsa_compress(cmp_rows[:, :, 1], pe_v, w1_v, w2_v, None)
    o_c, o_s = _nsa_cmp_slc(q, q_rot, pos, kc, vc, slc_rows[:, :, 0], slc_rows[:, :, 1], prec)
    g = jax.nn.sigmoid(gate_logits).reshape(b, t, NSA_HEADS, 3)
    o = g[..., 0:1] * o_c + g[..., 1:2] * o_s + g[..., 2:3] * o_win
    return o.reshape(b, t, NSA_W)


def _hgrn2_chunked(q, k, v, logf, s0, prec):
    b, t, h, dk = q.shape
    dv = v.shape[-1]
    c = math.gcd(t, HG_CHUNK)
    n = t // c

    def time_major(a):
        return a.reshape(b, n, c, h, a.shape[-1]).transpose(1, 0, 3, 2, 4)

    causal = jnp.arange(c)[:, None] >= jnp.arange(c)[None, :]

    def step(S, inp):
        qc, kc, vc, gc = inp
        bb = jnp.cumsum(gc, axis=2)
        inter = jnp.einsum('bhtd,bhde->bhte', qc * jnp.exp(bb), S, precision=prec)
        diff = bb[:, :, :, None, :] - bb[:, :, None, :, :]
        decay = jnp.exp(jnp.where(causal[:, :, None], diff, -jnp.inf))
        A = jnp.einsum('bhtd,bhsd,bhtsd->bhts', qc, kc, decay, precision=prec)
        o = inter + jnp.einsum('bhts,bhse->bhte', A, vc, precision=prec)
        b_last = bb[:, :, -1:, :]
        S = jnp.exp(b_last[:, :, 0, :])[..., None] * S + jnp.einsum(
            'bhsd,bhse->bhde', kc * jnp.exp(b_last - bb), vc, precision=prec)
        return S, o

    S, o = lax.scan(step, s0, (time_major(q), time_major(k), time_major(v), time_major(logf)))
    return o.transpose(1, 0, 3, 2, 4).reshape(b, t, h, dv), S


def _hgrn2_mixer(hq, hf, hv, hg, lb, g_norm, s0, prec):
    b, t = hq.shape[:2]
    q = hq.reshape(b, t, HG_HEADS, HG_DK)
    lbh = lb.reshape(HG_HEADS, HG_DK)
    f = lbh + (1.0 - lbh) * jax.nn.sigmoid(hf.reshape(b, t, HG_HEADS, HG_DK))
    logf = jnp.log(f)
    k = 1.0 - f
    v = hv.reshape(b, t, HG_HEADS, HG_DV)
    o, S = _hgrn2_chunked(q, k, v, logf, s0, prec)
    gn = g_norm.reshape(HG_HEADS, HG_DV)
    o = o * lax.rsqrt(jnp.mean(o * o, axis=-1, keepdims=True) + NORM_EPS) * gn
    o = o.reshape(b, t, HG_VW) * jax.nn.silu(hg)
    return o, S


def _split_proj(p, b, t, pos):
    q = p[:, COL_Q:COL_Q + NSA_W].reshape(b, t, NSA_HEADS, HD)
    q_rot = _rope_partial(q, pos)
    kv6 = p[:, COL_KV:COL_KV + 6 * NSA_KV * HD].reshape(b, t, 6, NSA_KV, HD)
    cmp_rows = kv6[:, :, 0:2]
    slc_rows = jnp.stack([_rope_partial(kv6[:, :, 2], pos), kv6[:, :, 3]], axis=2)
    win_rows = jnp.stack([_rope_partial(kv6[:, :, 4], pos), kv6[:, :, 5]], axis=2)
    ng = p[:, COL_NG:COL_NG + N_GATE].reshape(b, t, N_GATE)
    hq = p[:, COL_HQ:COL_HQ + HG_KW].reshape(b, t, HG_KW)
    hf = p[:, COL_HF:COL_HF + HG_KW].reshape(b, t, HG_KW)
    hv = p[:, COL_HV:COL_HV + HG_VW].reshape(b, t, HG_VW)
    hg = p[:, COL_HGATE:COL_HGATE + HG_VW].reshape(b, t, HG_VW)
    return q, q_rot, cmp_rows, slc_rows, win_rows, ng, (hq, hf, hv, hg)


def _paged_rows(pool, page_table):
    rows = pool[page_table]
    db, n_pages = page_table.shape
    return rows.reshape((db, n_pages * PAGE_SIZE) + rows.shape[3:])


def _reorder_w_in(w):
    o_q, o_kv = 0, NSA_W
    o_ng = o_kv + 6 * NSA_KV * HD
    o_hq = o_ng + N_GATE
    o_ga = o_hq + 2 * HG_KW + 2 * HG_VW
    pad = jnp.zeros((w.shape[0], D_IN_PAD - COL_NG - N_GATE), w.dtype)
    return jnp.concatenate([w[:, o_ga:o_ga + 2 * D_MODEL], w[:, o_q:o_ng], w[:, o_hq:o_ga],
                            w[:, o_ng:o_hq], pad], axis=1)


def _moe_group(top_idx, n_tok):
    n_assign = n_tok * TOP_K
    n_blk = -(-n_assign // MOE_ROWS) + N_EXPERTS
    n_rows = n_blk * MOE_ROWS
    e_flat = top_idx.reshape(-1)
    onehot = (e_flat[:, None] == jnp.arange(N_EXPERTS, dtype=jnp.int32)[None, :]).astype(jnp.int32)
    csum = jnp.cumsum(onehot, axis=0)
    rank = jnp.sum((csum - onehot) * onehot, axis=1)
    counts = csum[-1]
    padded = (counts + MOE_ROWS - 1) // MOE_ROWS * MOE_ROWS
    pend = jnp.cumsum(padded)
    pstart = pend - padded
    dest = pstart[e_flat] + rank
    blk_e = jnp.minimum(jnp.sum(jnp.arange(n_blk)[:, None] * MOE_ROWS >= pend[None, :], axis=1), N_EXPERTS - 1)
    order = jnp.argsort(e_flat, stable=True).astype(jnp.int32)
    start = jnp.cumsum(counts) - counts
    row_e = jnp.repeat(blk_e, MOE_ROWS)
    local = jnp.arange(n_rows, dtype=jnp.int32) - pstart[row_e]
    src = order[jnp.clip(start[row_e] + local, 0, n_assign - 1)]
    row_tok = jnp.where(local < counts[row_e], src // TOP_K, n_tok)
    nused = (pend[-1] // MOE_ROWS).reshape(1)
    return row_tok, dest.reshape(n_tok, TOP_K), blk_e.astype(jnp.int32), nused.astype(jnp.int32)


def kernel(x_prompt, x_sample, cache_cmp_kv, cache_slc_kv, cache_win_kv, state_hgrn, cache_mem_kv, page_table, mem_prompt, norm_mix, w_in, cmp_pe_k, cmp_w1_k, cmp_w2_k, cmp_pe_v, cmp_w1_v, cmp_w2_v, lb_logits, hg_norm, w_branch_a, w_branch_b, w_out, norm_x, wq_x, wk_x, wv_x, wo_x, norm_ffn, w_router, b_router, w_gu, b_gu, w_dn, b_dn, norm_final):
    bsz, seq, d = x_prompt.shape
    dbs, dec_seq, _ = x_sample.shape
    past = page_table.shape[1] * PAGE_SIZE
    win_buf_len = cache_win_kv.shape[2]
    n_p = bsz * seq
    n_s = dbs * dec_seq
    pos_p = jnp.arange(seq, dtype=jnp.int32)
    pos_s = past + jnp.arange(dec_seq, dtype=jnp.int32)
    kpos_w = jnp.arange(past - win_buf_len, past + dec_seq, dtype=jnp.int32)
    lb_all = jnp.cumsum(jax.nn.softmax(lb_logits, axis=0), axis=0)
    l = 0
    cmp_par = (cmp_pe_k[l], cmp_w1_k[l], cmp_w2_k[l], cmp_pe_v[l], cmp_w1_v[l], cmp_w2_v[l])
    w_in_r = _reorder_w_in(w_in[l])
    wa, wb, wo = w_branch_a[l], w_branch_b[l], w_out[l]
    xp = x_prompt.reshape(n_p, d)
    xs = x_sample.reshape(n_s, d)

    proj_p = _norm_matmul(xp, norm_mix[l], w_in_r.astype(BF16), normalize=True, hp=False, tm=1024, tn=PROJ_TN)
    o_nsa, cmp_rows, slc_rows, win_rows = _nsa_prompt(proj_p, bsz, seq, pos_p, cmp_par)
    o_hg, s_final = _hgrn2_prompt(proj_p, bsz, seq, lb_all[l], hg_norm[l])
    xp1 = _merge(xp, o_nsa.reshape(n_p, NSA_W), o_hg, proj_p,
                 wa.astype(BF16), wb.astype(BF16), wo.astype(BF16), hp=False, tm=256)

    wkv = jnp.concatenate([wk_x[l], wv_x[l]], axis=1).astype(BF16)
    mem2d = mem_prompt.reshape(bsz * mem_prompt.shape[1], d)
    mkv = _norm_matmul(mem2d, norm_x[l], wkv, normalize=False, hp=False, tm=mem2d.shape[0], tn=PROJ_TN)
    mkv = mkv.reshape(bsz, mem_prompt.shape[1], 2 * X_W)
    xp2 = _cross(xp1.reshape(bsz, seq, d), norm_x[l], mkv, wq_x[l].astype(BF16), wo_x[l].astype(BF16), tm=512)
    xp2 = xp2.reshape(n_p, d)

    proj_s = _norm_matmul(xs, norm_mix[l], w_in_r, normalize=True, hp=True, tm=n_s, tn=PROJ_TN)
    assert dec_seq == 1
    o_nsa_s, cmp_new, slc_new, win_new = _nsa_decode(proj_s, past, cache_cmp_kv[l], cache_slc_kv[l],
                                                      cache_win_kv[l], page_table, cmp_par)
    win_keep = jnp.concatenate([cache_win_kv[l], win_new], axis=1)[:, dec_seq:]
    o_hg_s, s_new = _hgrn2_step(proj_s, state_hgrn[l], lb_all[l], hg_norm[l])
    xs1 = _merge(xs, o_nsa_s, o_hg_s, proj_s, wa, wb, wo, hp=True, tm=n_s)

    hs = _rms(xs1, norm_x[l]).reshape(dbs, dec_seq, d)
    mem_s = cache_mem_kv[l]
    qx = jnp.einsum('btd,dw->btw', hs, wq_x[l], precision=HI).reshape(dbs, dec_seq, X_HEADS, X_HD)
    sx = jnp.einsum('bthd,bmhd->bhtm', qx, mem_s[:, :, 0], precision=HI) * X_SCALE
    px = jax.nn.softmax(sx, axis=-1)
    ox = jnp.einsum('bhtm,bmhd->bthd', px, mem_s[:, :, 1], precision=HI).reshape(n_s, X_W)
    xs2 = xs1 + jnp.dot(ox, wo_x[l], precision=HI)

    n_tok = n_p + n_s
    w_r_pad = jnp.pad(w_router[l], ((0, 0), (0, LANES - N_EXPERTS)))
    b_r_pad = jnp.pad(b_router[l], (0, LANES - N_EXPERTS)).reshape(1, LANES)
    h_p, val_p, idx_p = _router(xp2, norm_ffn[l], w_r_pad, b_r_pad, tm=512)
    h_s, val_s, idx_s = _router(xs2, norm_ffn[l], w_r_pad, b_r_pad, tm=n_s)
    top_val = jnp.concatenate([val_p, val_s], axis=0)[:, :TOP_K]
    top_idx = jnp.concatenate([idx_p, idx_s], axis=0)[:, :TOP_K]
    gates = jax.nn.softmax(top_val, axis=-1)
    row_tok, pos, blk_e, nused = _moe_group(top_idx, n_tok)
    h_all = jnp.concatenate([h_p, h_s, jnp.zeros((1, d), BF16)], axis=0)
    xb = h_all[row_tok]
    yb = _moe_experts(xb, blk_e, nused, w_gu[l], b_gu[l], w_dn[l], b_dn[l])
    yg = yb[pos.T]
    y_prompt = _combine(xp2, yg, gates, norm_final, tm=256, row0=0)
    y_sample = _combine(xs2, yg, gates, norm_final, tm=n_s, row0=n_p)

    mkv_out = mkv.reshape(bsz, mem_prompt.shape[1], 2, X_HEADS, X_HD)
    return (y_prompt.reshape(bsz, seq, d), y_sample.reshape(dbs, dec_seq, d),
            cmp_rows[None], slc_rows[None], win_rows[:, seq - min(WINDOW, seq):][None], s_final[None],
            mkv_out[None], cmp_new[None], slc_new[None],
            win_keep[None], s_new[None])
```
